```python
import jax, jax.numpy as jnp
from jax import lax
import numpy as np

D_MODEL = 1024
BATCH = 4
SEQ = 8192
DEPTH = 1

ATT_GROUPS = ((128, 1), (512, 4), (2048, 16))
HEADS_PER_GROUP = 4
N_ATT_HEADS = HEADS_PER_GROUP * len(ATT_GROUPS)
ATT_HEAD_DIM = 64
ATT_WIDTH = N_ATT_HEADS * ATT_HEAD_DIM
ATT_BLOCK = 64
NEG_INF = -1e30
GLA_HEADS = 4
GLA_DK = 64
GLA_DV = 128
GLA_KEY_WIDTH = GLA_HEADS * GLA_DK
GLA_VAL_WIDTH = GLA_HEADS * GLA_DV
GLA_GATE_RANK = 16
GLA_TAU = 16.0
GLA_CHUNK = 64
N_DIRECTIONS = 2
REL_BUCKETS = 32
REL_MAX_DISTANCE = 1024
DEEPNORM_ALPHA = (2.0 * DEPTH) ** 0.25
DEEPNORM_BETA = (8.0 * DEPTH) ** -0.25
LN_EPS = 1e-5
RMS_EPS = 1e-6

_IN_SPLITS = (ATT_WIDTH, ATT_WIDTH, ATT_WIDTH, ATT_WIDTH,
              GLA_KEY_WIDTH, GLA_KEY_WIDTH, GLA_VAL_WIDTH, GLA_VAL_WIDTH,
              GLA_GATE_RANK, GLA_GATE_RANK,
              D_MODEL, D_MODEL)
_IN_WIDTH = sum(_IN_SPLITS)
_SPLIT_POINTS = tuple(int(v) for v in np.cumsum(_IN_SPLITS)[:-1])
_VALUE_SEGMENTS = (2, 6)

kernel_name = "hybrid_dilated_gla_encoder_layer"


def _t5_bucket(rel):
    half = REL_BUCKETS // 2
    max_exact = half // 2
    ret = (rel > 0).astype(np.int32) * half
    n = np.abs(rel)
    large = max_exact + (np.log(np.maximum(n, 1) / max_exact)
                         / np.log(REL_MAX_DISTANCE / max_exact) * (half - max_exact)).astype(np.int32)
    large = np.minimum(large, half - 1)
    return ret + np.where(n < max_exact, n, large)


def _dilated_window_attention(q, k, v, bias_table, window, dilation):
    bsz, seq, nh, hd = q.shape
    half = window // (2 * dilation)
    blk = ATT_BLOCK
    L = seq // dilation
    nb = -(-L // blk)
    Lp = nb * blk

    def strided(t):
        t = t.reshape(bsz, L, dilation, nh, hd).transpose(0, 2, 3, 1, 4)
        return jnp.pad(t, ((0, 0), (0, 0), (0, 0), (0, Lp - L), (0, 0)))

    def band(t):
        t = jnp.pad(strided(t), ((0, 0), (0, 0), (0, 0), (blk, blk), (0, 0)))
        t = t.reshape(bsz, dilation, nh, nb + 2, blk, hd)
        return jnp.concatenate([t[:, :, :, :-2], t[:, :, :, 1:-1], t[:, :, :, 2:]], axis=-2)

    qs = strided(q).reshape(bsz, dilation, nh, nb, blk, hd)
    kb, vb = band(k), band(v)

    a = np.arange(blk)[:, None]
    c = np.arange(3 * blk)[None, :]
    dl = c - blk - a
    bias = jnp.transpose(bias_table[_t5_bucket(dl * dilation)], (2, 0, 1)).astype(jnp.float32)
    kpos = np.arange(nb)[:, None, None] * blk + (c - blk)[None]
    valid = (np.abs(dl)[None] <= half) & (kpos >= 0) & (kpos < L)

    logits = jnp.einsum('brhnqd,brhnkd->brhnqk', qs, kb).astype(jnp.float32) * (hd ** -0.5)
    logits = jnp.where(valid, logits + bias[:, None], NEG_INF)
    m = jnp.max(logits, axis=-1, keepdims=True)
    p = jnp.exp(logits - m)
    den = jnp.sum(p, axis=-1, keepdims=True)
    out = jnp.einsum('brhnqk,brhnkd->brhnqd', p, vb.astype(jnp.float32)) / den
    lse = (m + jnp.log(den))[..., 0]

    out = out.reshape(bsz, dilation, nh, Lp, hd)[:, :, :, :L]
    out = out.transpose(0, 3, 1, 2, 4).reshape(bsz, seq, nh, hd)
    lse = lse.reshape(bsz, dilation, nh, Lp)[:, :, :, :L]
    lse = lse.transpose(0, 3, 1, 2).reshape(bsz, seq, nh)
    return out, lse


def _gla_direction(q, k, v, log_a):
    bsz, seq, nh, dk = q.shape
    dv = v.shape[-1]
    C = GLA_CHUNK
    n = seq // C

    def chunk(t):
        return t.reshape(bsz, n, C, nh, t.shape[-1]).transpose(0, 3, 1, 2, 4)

    q, k, v, log_a = chunk(q), chunk(k), chunk(v), chunk(log_a)
    b = jnp.cumsum(log_a, axis=3)
    b_last = b[:, :, :, -1:]
    q_dec = q * jnp.exp(b)
    k_dec = k * jnp.exp(-b)
    lower = np.tril(np.ones((C, C), dtype=bool))
    scores = jnp.where(lower, jnp.einsum('bhnik,bhnjk->bhnij', q_dec, k_dec), 0.0)
    o_intra = jnp.einsum('bhnij,bhnjv->bhniv', scores, v)

    kv = jnp.einsum('bhnjk,bhnjv->bhnkv', k * jnp.exp(b_last - b), v)
    decay = jnp.exp(b_last[:, :, :, 0])

    def step(state, inp):
        dec, kv_c = inp
        return dec[..., None] * state + kv_c, state

    _, states = lax.scan(step, jnp.zeros((bsz, nh, dk, dv), q.dtype),
                         (jnp.moveaxis(decay, 2, 0), jnp.moveaxis(kv, 2, 0)))
    states = jnp.moveaxis(states, 0, 2)
    o = o_intra + jnp.einsum('bhnik,bhnkv->bhniv', q_dec, states)
    return o.transpose(0, 2, 3, 1, 4).reshape(bsz, seq, nh, dv)


def _hybrid_layer(x, w_in, gla_gate_w2, gla_gate_b, gla_norm_g, rel_bias,
                  w_att_out, w_gla_out, w_out, ln_g, ln_b):
    bsz, seq, _ = x.shape
    f32 = lambda t: t.astype(jnp.float32)
    proj = x @ w_in
    (qa, ka, va, ga, qb, kb, vb, gb, lr_f, lr_b, gate_att, gate_gla) = jnp.split(
        proj, _SPLIT_POINTS, axis=-1)

    qa = qa.reshape(bsz, seq, N_ATT_HEADS, ATT_HEAD_DIM)
    ka = ka.reshape(bsz, seq, N_ATT_HEADS, ATT_HEAD_DIM)
    va = va.reshape(bsz, seq, N_ATT_HEADS, ATT_HEAD_DIM)
    outs, lses = [], []
    for g, (window, dilation) in enumerate(ATT_GROUPS):
        sl = slice(g * HEADS_PER_GROUP, (g + 1) * HEADS_PER_GROUP)
        o, l = _dilated_window_attention(qa[:, :, sl], ka[:, :, sl], va[:, :, sl],
                                         rel_bias[:, sl], window, dilation)
        outs.append(o)
        lses.append(l)
    o_all = jnp.stack(outs, axis=2)
    w_grp = jax.nn.softmax(jnp.stack(lses, axis=2), axis=2)
    y_att = (o_all * w_grp[..., None]).reshape(bsz, seq, ATT_WIDTH).astype(x.dtype)
    y_att = (y_att * jax.nn.silu(ga)) @ w_att_out

    qg = f32(qb).reshape(bsz, seq, GLA_HEADS, GLA_DK) * (GLA_DK ** -0.5)
    kg = f32(kb).reshape(bsz, seq, GLA_HEADS, GLA_DK)
    vg = f32(vb).reshape(bsz, seq, GLA_HEADS, GLA_DV)

    def log_decay(lr, w2, bias):
        z = f32(lr) @ f32(w2) + f32(bias)
        return (jax.nn.log_sigmoid(z) / GLA_TAU).reshape(bsz, seq, GLA_HEADS, GLA_DK)

    flip = lambda t: t[:, ::-1]
    o_fwd = _gla_direction(qg, kg, vg, log_decay(lr_f, gla_gate_w2[0], gla_gate_b[0]))
    o_bwd = flip(_gla_direction(flip(qg), flip(kg), flip(vg),
                                flip(log_decay(lr_b, gla_gate_w2[1], gla_gate_b[1]))))
    o = o_fwd + o_bwd
    o = o * lax.rsqrt(jnp.mean(o * o, axis=-1, keepdims=True) + RMS_EPS)
    o = o.reshape(bsz, seq, GLA_VAL_WIDTH) * f32(gla_norm_g)
    y_gla = (o.astype(x.dtype) * jax.nn.silu(gb)) @ w_gla_out

    merged = jax.nn.sigmoid(gate_att) * y_att + jax.nn.sigmoid(gate_gla) * y_gla
    h = f32(DEEPNORM_ALPHA * x + merged @ w_out)
    mu = jnp.mean(h, axis=-1, keepdims=True)
    var = jnp.mean(jnp.square(h - mu), axis=-1, keepdims=True)
    y = (h - mu) * lax.rsqrt(var + LN_EPS) * f32(ln_g) + f32(ln_b)
    return y.astype(x.dtype)


def setup_inputs(seed: int = 0) -> dict:
    key = jax.random.key(seed)
    ks = jax.random.split(key, 12)
    nrm = lambda k, shape, scale: jax.random.normal(k, shape, jnp.float32) * scale
    col_scale = jnp.concatenate([
        jnp.full((n,), DEEPNORM_BETA if i in _VALUE_SEGMENTS else 1.0, jnp.float32)
        for i, n in enumerate(_IN_SPLITS)])
    return {
        "x": nrm(ks[0], (BATCH, SEQ, D_MODEL), 1.0),
        "w_in": nrm(ks[1], (DEPTH, D_MODEL, _IN_WIDTH), D_MODEL ** -0.5) * col_scale,
        "gla_gate_w2": nrm(ks[2], (DEPTH, N_DIRECTIONS, GLA_GATE_RANK, GLA_KEY_WIDTH), GLA_GATE_RANK ** -0.5),
        "gla_gate_b": nrm(ks[3], (DEPTH, N_DIRECTIONS, GLA_KEY_WIDTH), 0.1),
        "gla_norm_g": 1.0 + nrm(ks[4], (DEPTH, GLA_VAL_WIDTH), 0.01),
        "rel_bias": nrm(ks[5], (REL_BUCKETS, N_ATT_HEADS), 0.5),
        "w_att_out": nrm(ks[6], (DEPTH, ATT_WIDTH, D_MODEL), ATT_WIDTH ** -0.5 * DEEPNORM_BETA),
        "w_gla_out": nrm(ks[7], (DEPTH, GLA_VAL_WIDTH, D_MODEL), GLA_VAL_WIDTH ** -0.5 * DEEPNORM_BETA),
        "w_out": nrm(ks[8], (DEPTH, D_MODEL, D_MODEL), D_MODEL ** -0.5 * DEEPNORM_BETA),
        "ln_g": 1.0 + nrm(ks[9], (DEPTH, D_MODEL), 0.01),
        "ln_b": nrm(ks[10], (DEPTH, D_MODEL), 0.01),
    }


def reference(x, w_in, gla_gate_w2, gla_gate_b, gla_norm_g, rel_bias,
              w_att_out, w_gla_out, w_out, ln_g, ln_b):
    for layer in range(DEPTH):
        x = _hybrid_layer(x, w_in[layer], gla_gate_w2[layer], gla_gate_b[layer],
                          gla_norm_g[layer], rel_bias, w_att_out[layer], w_gla_out[layer],
                          w_out[layer], ln_g[layer], ln_b[layer])
    return x
```

```python
import functools

import numpy as np
import jax
import jax.numpy as jnp
from jax import lax
from jax.experimental import pallas as pl
from jax.experimental.pallas import tpu as pltpu

F32 = jnp.float32
BF16 = jnp.bfloat16

D_MODEL = 1024
ATT_GROUPS = ((128, 1), (512, 4), (2048, 16))
HEADS_PER_GROUP = 4
ATT_HEAD_DIM = 64
GROUP_WIDTH = HEADS_PER_GROUP * ATT_HEAD_DIM
ATT_WIDTH = GROUP_WIDTH * len(ATT_GROUPS)
ATT_HALF = 64
NEG_INF = -1e30
GLA_HEADS = 4
GLA_DK = 64
GLA_DV = 128
GLA_KEY_WIDTH = GLA_HEADS * GLA_DK
GLA_VAL_WIDTH = GLA_HEADS * GLA_DV
GLA_GATE_RANK = 16
GLA_TAU = 16.0
GLA_CHUNK = 64
REL_BUCKETS = 32
REL_MAX_DISTANCE = 1024
DEPTH = 1
DEEPNORM_ALPHA = (2.0 * DEPTH) ** 0.25
LN_EPS = 1e-5
RMS_EPS = 1e-6

_C_QA, _C_KA, _C_VA, _C_GA = 0, 768, 1536, 2304
_C_QB, _C_KB, _C_VB, _C_GB = 3072, 3328, 3584, 4096
_C_LR, _C_GATES, _C_END = 4608, 4640, 6688
LR_PAD = 128

ATT_QBLK = 128
ATT_KWIN = 256
MAX_DIL = 16

VMEM_LIMIT = 52 * 1024 * 1024


def _params(n_axes):
    return pltpu.CompilerParams(dimension_semantics=("arbitrary",) * n_axes,
                                vmem_limit_bytes=VMEM_LIMIT)


def _proj_body(x_ref, w_ref, *o_refs, widths):
    x = x_ref[...].astype(BF16)
    off = 0
    for o_ref, n in zip(o_refs, widths):
        for c in range(0, n, 512):
            w = min(512, n - c)
            acc = jnp.dot(x, w_ref[:, off + c:off + c + w], preferred_element_type=F32)
            o_ref[:, c:c + w] = acc.astype(o_ref.dtype)
        off += n


def _project(x, w, dil, tm, widths, dtypes, name):
    bsz, seq, dm = x.shape
    sub = seq // dil
    xv = x.reshape(bsz, sub, dil * dm)
    ncols = sum(widths)
    return pl.pallas_call(
        functools.partial(_proj_body, widths=widths),
        grid=(bsz, dil, sub // tm),
        in_specs=[pl.BlockSpec((None, tm, dm), lambda b, r, i: (b, i, r)),
                  pl.BlockSpec((dm, ncols), lambda b, r, i: (0, 0))],
        out_specs=[pl.BlockSpec((None, None, tm, n), lambda b, r, i: (b, r, i, 0)) for n in widths],
        out_shape=[jax.ShapeDtypeStruct((bsz, dil, sub, n), dt) for n, dt in zip(widths, dtypes)],
        compiler_params=_params(3),
        name=name,
    )(xv, w)


def _t5_bucket(rel):
    half = REL_BUCKETS // 2
    max_exact = half // 2
    ret = (rel > 0).astype(np.int32) * half
    n = np.abs(rel)
    large = max_exact + (np.log(np.maximum(n, 1) / max_exact)
                         / np.log(REL_MAX_DISTANCE / max_exact) * (half - max_exact)).astype(np.int32)
    large = np.minimum(large, half - 1)
    return ret + np.where(n < max_exact, n, large)


def _attn_bias(rel_bias_g, dilation):
    a = np.arange(ATT_QBLK)[:, None]
    c = np.arange(ATT_KWIN)[None, :]
    kinds = []
    for shift in (0, ATT_HALF, 2 * ATT_HALF):
        dl = c - shift - a
        bucket = _t5_bucket(dl * dilation)
        valid = np.abs(dl) <= ATT_HALF
        tile = jnp.where(valid[:, :, None], rel_bias_g[bucket].astype(F32), NEG_INF)
        tile = jnp.transpose(tile, (2, 0, 1)).reshape(2, 2 * ATT_QBLK, ATT_KWIN)
        kinds.append(tile)
    return jnp.stack(kinds, axis=0)


def _attn_body(q_ref, k_ref, v_ref, bias_ref, hm_ref, o_ref, lse_ref, *, sub, tq):
    qt = pl.program_id(2)
    lo = lax.broadcasted_iota(jnp.int32, (ATT_QBLK, 128), 1) < ATT_HEAD_DIM
    hm0 = hm_ref[0:1, :]
    hm1 = hm_ref[1:2, :]

    def step(j, carry):
        q0 = qt * tq + j * ATT_QBLK
        ws = pl.multiple_of(jnp.clip(q0 - ATT_HALF, 0, sub - ATT_KWIN), ATT_HALF)
        kind = jnp.where(q0 == 0, 0, jnp.where(q0 == sub - ATT_QBLK, 2, 1))
        r0 = pl.multiple_of(j * ATT_QBLK, ATT_QBLK)
        for p in range(2):
            cols = slice(128 * p, 128 * p + 128)
            q = q_ref[pl.ds(r0, ATT_QBLK), cols]
            qs = jnp.concatenate([q * hm0, q * hm1], axis=0)
            k = k_ref[pl.ds(ws, ATT_KWIN), cols]
            v = v_ref[pl.ds(ws, ATT_KWIN), cols]
            s = lax.dot_general(qs, k, (((1,), (1,)), ((), ())), preferred_element_type=F32)
            s = s + bias_ref[kind, p]
            m = jnp.max(s, axis=-1, keepdims=True)
            e = jnp.exp(s - m)
            den = jnp.sum(e, axis=-1, keepdims=True)
            pv = jnp.dot(e.astype(BF16), v, preferred_element_type=F32) / den
            lse = m + jnp.log(den)
            o_ref[pl.ds(r0, ATT_QBLK), cols] = jnp.where(lo, pv[:ATT_QBLK], pv[ATT_QBLK:]).astype(o_ref.dtype)
            lse_ref[pl.ds(r0, ATT_QBLK), cols] = jnp.where(lo, lse[:ATT_QBLK], lse[ATT_QBLK:])
        return carry

    lax.fori_loop(0, tq // ATT_QBLK, step, 0)


def _attention(qkv, bias, hmask, tq, name):
    bsz, dil, sub, _ = qkv.shape
    gw = GROUP_WIDTH
    return pl.pallas_call(
        functools.partial(_attn_body, sub=sub, tq=tq),
        grid=(bsz, dil, sub // tq),
        in_specs=[pl.BlockSpec((None, None, tq, gw), lambda b, r, i: (b, r, i, 0)),
                  pl.BlockSpec((None, None, sub, gw), lambda b, r, i: (b, r, 0, 1)),
                  pl.BlockSpec((None, None, sub, gw), lambda b, r, i: (b, r, 0, 2)),
                  pl.BlockSpec(bias.shape, lambda b, r, i: (0, 0, 0, 0)),
                  pl.BlockSpec(hmask.shape, lambda b, r, i: (0, 0))],
        out_specs=[pl.BlockSpec((None, None, tq, gw), lambda b, r, i: (b, r, i, 0)),
                   pl.BlockSpec((None, None, tq, gw), lambda b, r, i: (b, r, i, 0))],
        out_shape=[jax.ShapeDtypeStruct((bsz, dil, sub, gw), BF16),
                   jax.ShapeDtypeStruct((bsz, dil, sub, gw), F32)],
        compiler_params=_params(3),
        name=name,
    )(qkv, qkv, qkv, bias, hmask)


def _gla_body(*refs, reverse, final, tile):
    if final:
        (qk_ref, v_ref, lr_ref, w2_ref, gbias_ref, tri_ref, hm_ref, gate_ref, obwd_ref, gnorm_ref,
         out_ref, st_ref, stb_ref, kbd_ref, vbd_ref) = refs
    else:
        (qk_ref, v_ref, lr_ref, w2_ref, gbias_ref, tri_ref, hm_ref,
         out_ref, st_ref, stb_ref, kbd_ref, vbd_ref) = refs
    C = GLA_CHUNK
    n_chunks = tile // C

    @pl.when(pl.program_id(1) == 0)
    def _():
        st_ref[...] = jnp.zeros_like(st_ref)
        stb_ref[...] = jnp.zeros_like(stb_ref)
        vbd_ref[...] = jnp.zeros_like(vbd_ref)

    row = lax.broadcasted_iota(jnp.int32, (C, GLA_KEY_WIDTH), 0)
    col = lax.broadcasted_iota(jnp.int32, (C, GLA_KEY_WIDTH), 1) & (C - 1)
    causal = (col >= row) if reverse else (col <= row)

    def step(i, carry):
        c = (n_chunks - 1 - i) if reverse else i
        rows = pl.ds(pl.multiple_of(c * C, C), C)
        q = qk_ref[rows, 0:GLA_KEY_WIDTH].astype(F32) * (GLA_DK ** -0.5)
        k = qk_ref[rows, GLA_KEY_WIDTH:2 * GLA_KEY_WIDTH].astype(F32)
        v = v_ref[rows, :]
        z = jnp.dot(lr_ref[rows, :].astype(BF16), w2_ref[...], preferred_element_type=F32) + gbias_ref[...]
        log_a = -(jnp.maximum(-z, 0.0) + jnp.log1p(jnp.exp(-jnp.abs(z)))) * (1.0 / GLA_TAU)
        a0 = log_a.astype(BF16)
        r1 = log_a - a0.astype(F32)
        a1 = r1.astype(BF16)
        a2 = (r1 - a1.astype(F32)).astype(BF16)
        cs = jnp.dot(tri_ref[...], jnp.concatenate([a0, a1, a2], axis=1), preferred_element_type=F32)
        b = cs[:, 0:256] + cs[:, 256:512] + cs[:, 512:768]
        b_last = b[0:1, :] if reverse else b[C - 1:C, :]
        q_dec = (q * jnp.exp(b)).astype(BF16)
        k_dec = (k * jnp.exp(-b)).astype(BF16)
        k_rem = (k * jnp.exp(b_last - b)).astype(BF16)
        decay = jnp.exp(b_last)
        for h in range(GLA_HEADS):
            kbd_ref[C * h:C * h + C, :] = k_dec * hm_ref[h:h + 1, :]
            vbd_ref[C * h:C * h + C, GLA_DV * h:GLA_DV * h + GLA_DV] = v[:, GLA_DV * h:GLA_DV * h + GLA_DV]
        scores = lax.dot_general(q_dec, kbd_ref[...], (((1,), (1,)), ((), ())), preferred_element_type=F32)
        scores = jnp.where(causal, scores, 0.0).astype(BF16)
        o = jnp.dot(scores, vbd_ref[...], preferred_element_type=F32)
        o = o + lax.dot_general(q_dec, stb_ref[...], (((1,), (1,)), ((), ())), preferred_element_type=F32)
        for h in range(GLA_HEADS):
            lb = 128 * (h // 2)
            kr = (k_rem * hm_ref[h:h + 1, :])[:, lb:lb + 128]
            kv_t = lax.dot_general(v[:, GLA_DV * h:GLA_DV * h + GLA_DV], kr, (((0,), (0,)), ((), ())),
                                   preferred_element_type=F32)
            blk = (slice(GLA_DV * h, GLA_DV * h + GLA_DV), slice(lb, lb + 128))
            new = decay[:, lb:lb + 128] * st_ref[blk] + kv_t
            st_ref[blk] = new
            stb_ref[blk] = new.astype(BF16)
        if final:
            o = o + obwd_ref[rows, :]
            parts = []
            for h in range(GLA_HEADS):
                oh = o[:, GLA_DV * h:GLA_DV * h + GLA_DV]
                ms = jnp.mean(oh * oh, axis=-1, keepdims=True)
                parts.append(oh * lax.rsqrt(ms + RMS_EPS))
            o = jnp.concatenate(parts, axis=1) * gnorm_ref[...]
            g = gate_ref[rows, :].astype(F32)
            out_ref[rows, :] = (o * (g / (1.0 + jnp.exp(-g)))).astype(out_ref.dtype)
        else:
            out_ref[rows, :] = o
        return carry

    lax.fori_loop(0, n_chunks, step, 0)


def _gla_pass(qk, v, lr, w2p, gbias, tri, hmask, extra, *, reverse, final, bsz, tile, name):
    n_tok = qk.shape[0]
    n_tiles = n_tok // bsz // tile
    if reverse:
        rmap = lambda b, i: (b * n_tiles + (n_tiles - 1 - i), 0)
    else:
        rmap = lambda b, i: (b * n_tiles + i, 0)
    cmap = lambda b, i: (0, 0)
    vw = GLA_VAL_WIDTH
    in_specs = [pl.BlockSpec((tile, 2 * GLA_KEY_WIDTH), rmap),
                pl.BlockSpec((tile, vw), rmap),
                pl.BlockSpec((tile, LR_PAD), rmap),
                pl.BlockSpec(w2p.shape, cmap), pl.BlockSpec(gbias.shape, cmap),
                pl.BlockSpec(tri.shape, cmap), pl.BlockSpec(hmask.shape, cmap)]
    args = [qk, v, lr, w2p, gbias, tri, hmask]
    if final:
        gate, obwd, gnorm = extra
        in_specs += [pl.BlockSpec((tile, vw), rmap), pl.BlockSpec((tile, vw), rmap),
                     pl.BlockSpec(gnorm.shape, cmap)]
        args += [gate, obwd, gnorm]
    return pl.pallas_call(
        functools.partial(_gla_body, reverse=reverse, final=final, tile=tile),
        grid=(bsz, n_tiles),
        in_specs=in_specs,
        out_specs=pl.BlockSpec((tile, vw), rmap),
        out_shape=jax.ShapeDtypeStruct((n_tok, vw), BF16 if final else F32),
        scratch_shapes=[pltpu.VMEM((GLA_VAL_WIDTH, GLA_KEY_WIDTH), F32),
                        pltpu.VMEM((GLA_VAL_WIDTH, GLA_KEY_WIDTH), BF16),
                        pltpu.VMEM((GLA_HEADS * GLA_CHUNK, GLA_KEY_WIDTH), BF16),
                        pltpu.VMEM((GLA_HEADS * GLA_CHUNK, GLA_VAL_WIDTH), BF16)],
        compiler_params=_params(2),
        name=name,
    )(*args)


def _final_body(x_ref, o0_ref, l0_ref, o1_ref, l1_ref, o2_ref, l2_ref, ga_ref, yg_ref, gates_ref,
                wa_ref, wg_ref, wo_ref, lng_ref, lnb_ref, out_ref):
    l0, l1, l2 = l0_ref[...], l1_ref[...], l2_ref[...]
    mx = jnp.maximum(jnp.maximum(l0, l1), l2)
    e0, e1, e2 = jnp.exp(l0 - mx), jnp.exp(l1 - mx), jnp.exp(l2 - mx)
    inv = 1.0 / (e0 + e1 + e2)
    y_att = jnp.concatenate([o0_ref[...].astype(F32) * (e0 * inv),
                             o1_ref[...].astype(F32) * (e1 * inv),
                             o2_ref[...].astype(F32) * (e2 * inv)], axis=1)
    ga = ga_ref[...].astype(F32)
    y_att = (y_att * (ga / (1.0 + jnp.exp(-ga)))).astype(BF16)
    ya = jnp.dot(y_att, wa_ref[...], preferred_element_type=F32)
    yg = jnp.dot(yg_ref[...], wg_ref[...], preferred_element_type=F32)
    g_att = gates_ref[:, 0:D_MODEL].astype(F32)
    g_gla = gates_ref[:, D_MODEL:2 * D_MODEL].astype(F32)
    merged = ya / (1.0 + jnp.exp(-g_att)) + yg / (1.0 + jnp.exp(-g_gla))
    h = DEEPNORM_ALPHA * x_ref[...] + jnp.dot(merged.astype(BF16), wo_ref[...], preferred_element_type=F32)
    mu = jnp.mean(h, axis=-1, keepdims=True)
    hc = h - mu
    var = jnp.mean(hc * hc, axis=-1, keepdims=True)
    out_ref[...] = hc * lax.rsqrt(var + LN_EPS) * lng_ref[...] + lnb_ref[...]


def _finalize(x, o0, l0, o1, l1, o2, l2, ga16, y_gla, gates16, wa, wg, wo, lng, lnb):
    bsz, seq, dm = x.shape
    md = MAX_DIL
    sub = seq // md
    gw = GROUP_WIDTH

    def strided(arr, width):
        return arr.reshape(bsz, sub, md * width), pl.BlockSpec((None, sub, width), lambda b, r: (b, 0, r))

    def strided4(arr, width):
        return (arr.reshape(bsz, 4, sub, 4 * width),
                pl.BlockSpec((None, None, sub, width), lambda b, r: (b, r % 4, 0, r // 4)))

    def slab16(arr, width):
        return arr, pl.BlockSpec((None, None, sub, width), lambda b, r: (b, r, 0, 0))

    def const(arr):
        return arr, pl.BlockSpec(arr.shape, lambda b, r: (0,) * arr.ndim)

    pairs = [strided(x, dm), strided(o0, gw), strided(l0, gw), strided4(o1, gw), strided4(l1, gw),
             slab16(o2, gw), slab16(l2, gw), slab16(ga16, ATT_WIDTH), strided(y_gla, GLA_VAL_WIDTH),
             slab16(gates16, 2 * dm), const(wa), const(wg), const(wo), const(lng), const(lnb)]
    out = pl.pallas_call(
        _final_body,
        grid=(bsz, md),
        in_specs=[s for _, s in pairs],
        out_specs=pl.BlockSpec((None, sub, dm), lambda b, r: (b, 0, r)),
        out_shape=jax.ShapeDtypeStruct((bsz, sub, md * dm), F32),
        compiler_params=_params(2),
        name="merge_out",
    )(*[a for a, _ in pairs])
    return out.reshape(bsz, seq, dm)


def _layer(x, w_in, gate_w2, gate_b, gla_norm_g, rel_bias, w_att_out, w_gla_out, w_out, ln_g, ln_b):
    bsz, seq, dm = x.shape
    gw = GROUP_WIDTH

    def qkv_cols(g):
        return [w_in[:, c + gw * g:c + gw * g + gw] for c in (_C_QA, _C_KA, _C_VA)]

    lr_cols = jnp.pad(w_in[:, _C_LR:_C_GATES], ((0, 0), (0, LR_PAD - 2 * GLA_GATE_RANK)))
    w_nat = jnp.concatenate(qkv_cols(0) + [w_in[:, _C_QB:_C_VB], w_in[:, _C_VB:_C_GB],
                                           w_in[:, _C_GB:_C_LR], lr_cols], axis=1).astype(BF16)
    w_d4 = jnp.concatenate(qkv_cols(1), axis=1).astype(BF16)
    w_d16 = jnp.concatenate(qkv_cols(2) + [w_in[:, _C_GA:_C_QB], w_in[:, _C_GATES:_C_END]], axis=1).astype(BF16)

    qkv0, qk, vv, gb, lr = _project(x, w_nat, 1, 1024, (ATT_WIDTH, 512, 512, 512, LR_PAD),
                                    (BF16, BF16, BF16, BF16, F32), "proj_nat")
    (qkv1,) = _project(x, w_d4, 4, 1024, (ATT_WIDTH,), (BF16,), "proj_d4")
    qkv2, ga16, gates16 = _project(x, w_d16, 16, 512, (ATT_WIDTH, ATT_WIDTH, 2 * dm),
                                   (BF16, BF16, BF16), "proj_d16")

    lane = np.arange(128)
    hmask_att = jnp.asarray(np.stack([lane < 64, lane >= 64]).astype(np.float32) * ATT_HEAD_DIM ** -0.5, BF16)
    outs = []
    for g, (qkv, tq) in enumerate(((qkv0, 1024), (qkv1, 1024), (qkv2, 512))):
        bias = _attn_bias(rel_bias[:, HEADS_PER_GROUP * g:HEADS_PER_GROUP * (g + 1)], ATT_GROUPS[g][1])
        outs.append(_attention(qkv, bias, hmask_att, tq, "attn_d%d" % ATT_GROUPS[g][1]))
    (o0, l0), (o1, l1), (o2, l2) = outs

    n_tok = bsz * seq
    qk2, vv2, gb2, lr2 = (t.reshape(n_tok, t.shape[-1]) for t in (qk, vv, gb, lr))
    lane = np.arange(GLA_KEY_WIDTH)
    hmask_gla = jnp.asarray((lane[None, :] // GLA_DK == np.arange(GLA_HEADS)[:, None]).astype(np.float32), BF16)
    ii = np.arange(GLA_CHUNK)
    tri_f = jnp.asarray((ii[None, :] <= ii[:, None]).astype(np.float32), BF16)
    tri_b = jnp.asarray((ii[None, :] >= ii[:, None]).astype(np.float32), BF16)

    def w2_padded(direction):
        lo = GLA_GATE_RANK * direction
        return jnp.pad(gate_w2[direction], ((lo, LR_PAD - lo - GLA_GATE_RANK), (0, 0))).astype(BF16)

    common = dict(bsz=bsz, tile=512)
    o_bwd = _gla_pass(qk2, vv2, lr2, w2_padded(1), gate_b[1][None, :].astype(F32), tri_b, hmask_gla, None,
                      reverse=True, final=False, name="gla_bwd", **common)
    y_gla = _gla_pass(qk2, vv2, lr2, w2_padded(0), gate_b[0][None, :].astype(F32), tri_f, hmask_gla,
                      (gb2, o_bwd, gla_norm_g[None, :].astype(F32)),
                      reverse=False, final=True, name="gla_fwd", **common)

    return _finalize(x, o0, l0, o1, l1, o2, l2, ga16, y_gla.reshape(bsz, seq, GLA_VAL_WIDTH), gates16,
                     w_att_out.astype(BF16), w_gla_out.astype(BF16), w_out.astype(BF16),
                     ln_g[None, :].astype(F32), ln_b[None, :].astype(F32))


def kernel(x, w_in, gla_gate_w2, gla_gate_b, gla_norm_g, rel_bias, w_att_out, w_gla_out, w_out, ln_g, ln_b):
    for layer in range(DEPTH):
        x = _layer(x, w_in[layer], gla_gate_w2[layer], gla_gate_b[layer], gla_norm_g[layer], rel_bias,
                   w_att_out[layer], w_gla_out[layer], w_out[layer], ln_g[layer], ln_b[layer])
    return x
```

```python
import functools

import numpy as np
import jax
import jax.numpy as jnp
from jax import lax
from jax.experimental import pallas as pl
from jax.experimental.pallas import tpu as pltpu

F32 = jnp.float32
BF16 = jnp.bfloat16

D_MODEL = 1024
ATT_GROUPS = ((128, 1), (512, 4), (2048, 16))
HEADS_PER_GROUP = 4
ATT_HEAD_DIM = 64
GROUP_WIDTH = HEADS_PER_GROUP * ATT_HEAD_DIM
ATT_WIDTH = GROUP_WIDTH * len(ATT_GROUPS)
ATT_HALF = 64
NEG_INF = -1e30
GLA_HEADS = 4
GLA_DK = 64
GLA_DV = 128
GLA_KEY_WIDTH = GLA_HEADS * GLA_DK
GLA_VAL_WIDTH = GLA_HEADS * GLA_DV
GLA_GATE_RANK = 16
GLA_TAU = 16.0
GLA_CHUNK = 64
REL_BUCKETS = 32
REL_MAX_DISTANCE = 1024
DEPTH = 1
DEEPNORM_ALPHA = (2.0 * DEPTH) ** 0.25
LN_EPS = 1e-5
RMS_EPS = 1e-6

_C_QA, _C_KA, _C_VA, _C_GA = 0, 768, 1536, 2304
_C_QB, _C_KB, _C_VB, _C_GB = 3072, 3328, 3584, 4096
_C_LR, _C_GATES, _C_END = 4608, 4640, 6688
LR_PAD = 128

ATT_QBLK = 128
ATT_KWIN = 256
BIAS_SPAN = 512
TOKEN_TILE = 512

VMEM_LIMIT = 56 * 1024 * 1024


def _params(n_axes):
    return pltpu.CompilerParams(dimension_semantics=("arbitrary",) * n_axes,
                                vmem_limit_bytes=VMEM_LIMIT)


def _split3(a):
    a0 = a.astype(BF16)
    r1 = a - a0.astype(F32)
    a1 = r1.astype(BF16)
    a2 = (r1 - a1.astype(F32)).astype(BF16)
    return a0, a1, a2


_NAT_WIDTHS = (ATT_WIDTH, 512, 512, 512, LR_PAD, ATT_WIDTH, 2 * D_MODEL)


def _proj_body(*refs, tm):
    n_xb = D_MODEL // 128
    x_refs, (wn_ref, w4_ref, w16_ref), o_refs = refs[:n_xb], refs[n_xb:n_xb + 3], refs[n_xb + 3:]
    nat_refs, qkv1_ref, qkv2_ref = o_refs[:-2], o_refs[-2], o_refs[-1]
    x = jnp.concatenate([xr[...].astype(BF16) for xr in x_refs], axis=1)
    off = 0
    for o_ref, n in zip(nat_refs, _NAT_WIDTHS):
        for c in range(0, n, 512):
            w = min(512, n - c)
            acc = jnp.dot(x, wn_ref[:, off + c:off + c + w], preferred_element_type=F32)
            o_ref[:, c:c + w] = acc.astype(o_ref.dtype)
        off += n
    for dil, w_ref, o_ref in ((4, w4_ref, qkv1_ref), (16, w16_ref, qkv2_ref)):
        n = tm // dil
        xs = jnp.concatenate(
            [jnp.concatenate([xr[pl.ds(r, n, stride=dil), :].astype(BF16) for xr in x_refs], axis=1)
             for r in range(dil)], axis=0)
        for c in range(0, ATT_WIDTH, GROUP_WIDTH):
            acc = jnp.dot(xs, w_ref[:, c:c + GROUP_WIDTH], preferred_element_type=F32).astype(o_ref.dtype)
            for r in range(dil):
                o_ref[r, :, c:c + GROUP_WIDTH] = acc[r * n:(r + 1) * n]


def _project(x, w_nat, w_d4, w_d16):
    bsz, seq, dm = x.shape
    tm = TOKEN_TILE
    nt = seq // tm
    n_tok = bsz * seq
    const = lambda w: pl.BlockSpec(w.shape, lambda b, i: (0, 0), pipeline_mode=pl.Buffered(1))
    nat_dtypes = (BF16, BF16, BF16, BF16, F32, BF16, BF16)
    out_specs = [pl.BlockSpec((tm, n), lambda b, i: (b * nt + i, 0)) for n in _NAT_WIDTHS]
    out_shape = [jax.ShapeDtypeStruct((n_tok, n), dt) for n, dt in zip(_NAT_WIDTHS, nat_dtypes)]
    for dil in (4, 16):
        out_specs.append(pl.BlockSpec((None, dil, tm // dil, ATT_WIDTH), lambda b, i: (b, 0, i, 0)))
        out_shape.append(jax.ShapeDtypeStruct((bsz, dil, seq // dil, ATT_WIDTH), BF16))
    x2 = x.reshape(n_tok, dm)
    x_specs = [pl.BlockSpec((tm, 128), lambda b, i, j=j: (b * nt + i, j)) for j in range(dm // 128)]
    return pl.pallas_call(
        functools.partial(_proj_body, tm=tm),
        grid=(bsz, nt),
        in_specs=x_specs + [const(w_nat), const(w_d4), const(w_d16)],
        out_specs=out_specs,
        out_shape=out_shape,
        compiler_params=_params(2),
        name="proj",
    )(*([x2] * len(x_specs)), w_nat, w_d4, w_d16)


def _t5_bucket(rel):
    half = REL_BUCKETS // 2
    max_exact = half // 2
    ret = (rel > 0).astype(np.int32) * half
    n = np.abs(rel)
    large = max_exact + (np.log(np.maximum(n, 1) / max_exact)
                         / np.log(REL_MAX_DISTANCE / max_exact) * (half - max_exact)).astype(np.int32)
    large = np.minimum(large, half - 1)
    return ret + np.where(n < max_exact, n, large)


def _bucket_onehot():
    m = np.arange(BIAS_SPAN)
    dl = np.where(m < BIAS_SPAN // 2, m, m - BIAS_SPAN)
    out = np.zeros((len(ATT_GROUPS), REL_BUCKETS, BIAS_SPAN), np.float32)
    for g, (_, dilation) in enumerate(ATT_GROUPS):
        out[g, _t5_bucket(dl * dilation), m] = 1.0
    return out


def _bias_body(rb_ref, oh_ref, out_ref):
    oh = oh_ref[...]
    table = sum(jnp.dot(t, oh, preferred_element_type=F32) for t in _split3(rb_ref[...]))
    m = lax.broadcasted_iota(jnp.int32, table.shape, 1)
    table = jnp.where((m <= ATT_HALF) | (m >= BIAS_SPAN - ATT_HALF), table, NEG_INF)
    for h in range(HEADS_PER_GROUP):
        rows = jnp.broadcast_to(table[h:h + 1, :], (ATT_QBLK, BIAS_SPAN))
        for kind in range(3):
            tile = pltpu.roll(rows, kind * ATT_HALF, 1, stride=1, stride_axis=0)
            out_ref[kind, h // 2, ATT_QBLK * (h % 2):ATT_QBLK * (h % 2 + 1), :] = tile[:, :ATT_KWIN]


def _attn_bias(rel_bias):
    n_g = len(ATT_GROUPS)
    rb = jnp.transpose(rel_bias.astype(F32)).reshape(n_g, HEADS_PER_GROUP, REL_BUCKETS)
    rb = jnp.pad(rb, ((0, 0), (0, 8 - HEADS_PER_GROUP), (0, 0)))
    onehot = jnp.asarray(_bucket_onehot(), BF16)
    return pl.pallas_call(
        _bias_body,
        grid=(n_g,),
        in_specs=[pl.BlockSpec((None, 8, REL_BUCKETS), lambda g: (g, 0, 0)),
                  pl.BlockSpec((None, REL_BUCKETS, BIAS_SPAN), lambda g: (g, 0, 0))],
        out_specs=pl.BlockSpec((None, 3, 2, 2 * ATT_QBLK, ATT_KWIN), lambda g: (g, 0, 0, 0, 0)),
        out_shape=jax.ShapeDtypeStruct((n_g, 3, 2, 2 * ATT_QBLK, ATT_KWIN), F32),
        compiler_params=_params(1),
        name="attn_bias",
    )(rb, onehot)


def _attn_body(q_ref, k_ref, v_ref, bias_ref, hm_ref, o_ref, lse_ref, *, sub, tq, unroll):
    qt = pl.program_id(2)
    lo = lax.broadcasted_iota(jnp.int32, (ATT_QBLK, 128), 1) < ATT_HEAD_DIM
    hm0 = hm_ref[0:1, :]
    hm1 = hm_ref[1:2, :]

    def step(j, carry):
        q0 = qt * tq + j * ATT_QBLK
        ws = pl.multiple_of(jnp.clip(q0 - ATT_HALF, 0, sub - ATT_KWIN), ATT_HALF)
        kind = jnp.where(q0 == 0, 0, jnp.where(q0 == sub - ATT_QBLK, 2, 1))
        r0 = pl.multiple_of(j * ATT_QBLK, ATT_QBLK)
        for p in range(2):
            cols = slice(128 * p, 128 * p + 128)
            q = q_ref[pl.ds(r0, ATT_QBLK), cols]
            qs = jnp.concatenate([q * hm0, q * hm1], axis=0)
            k = k_ref[pl.ds(ws, ATT_KWIN), cols]
            v = v_ref[pl.ds(ws, ATT_KWIN), cols]
            s = lax.dot_general(qs, k, (((1,), (1,)), ((), ())), preferred_element_type=F32)
            s = s + bias_ref[kind, p]
            m = jnp.max(s, axis=-1, keepdims=True)
            e = jnp.exp(s - m)
            den = jnp.sum(e, axis=-1, keepdims=True)
            pv = jnp.dot(e.astype(BF16), v, preferred_element_type=F32) / den
            lse = m + jnp.log(den)
            o_ref[pl.ds(r0, ATT_QBLK), cols] = jnp.where(lo, pv[:ATT_QBLK], pv[ATT_QBLK:]).astype(o_ref.dtype)
            lse_ref[pl.ds(r0, ATT_QBLK), cols] = jnp.where(lo, lse[:ATT_QBLK], lse[ATT_QBLK:])
        return carry

    lax.fori_loop(0, tq // ATT_QBLK, step, 0, unroll=unroll)


def _attention(qkv, bias_all, g, hmask, tq, name):
    bsz, dil, sub, _ = qkv.shape
    gw = GROUP_WIDTH
    return pl.pallas_call(
        functools.partial(_attn_body, sub=sub, tq=tq, unroll=2),
        grid=(bsz, dil, sub // tq),
        in_specs=[pl.BlockSpec((None, None, tq, gw), lambda b, r, i: (b, r, i, 0)),
                  pl.BlockSpec((None, None, sub, gw), lambda b, r, i: (b, r, 0, 1)),
                  pl.BlockSpec((None, None, sub, gw), lambda b, r, i: (b, r, 0, 2)),
                  pl.BlockSpec((None,) + bias_all.shape[1:], lambda b, r, i: (g, 0, 0, 0, 0)),
                  pl.BlockSpec(hmask.shape, lambda b, r, i: (0, 0))],
        out_specs=[pl.BlockSpec((None, None, tq, gw), lambda b, r, i: (b, r, i, 0)),
                   pl.BlockSpec((None, None, tq, gw), lambda b, r, i: (b, r, i, 0))],
        out_shape=[jax.ShapeDtypeStruct((bsz, dil, sub, gw), BF16),
                   jax.ShapeDtypeStruct((bsz, dil, sub, gw), F32)],
        compiler_params=_params(3),
        name=name,
    )(qkv, qkv, qkv, bias_all, hmask)


def _gla_body(*refs, reverse, final, tile):
    if final:
        (qk_ref, v_ref, lr_ref, w2_ref, gbias_ref, tri_ref, hm_ref, gate_ref, obwd_ref, gnorm_ref,
         out_ref, st_ref, stb_ref, vbd_ref, la_ref) = refs
    else:
        (qk_ref, v_ref, lr_ref, w2_ref, gbias_ref, tri_ref, hm_ref,
         out_ref, st_ref, stb_ref, vbd_ref, la_ref) = refs
    C = GLA_CHUNK
    n_chunks = tile // C

    @pl.when(pl.program_id(1) == 0)
    def _():
        st_ref[...] = jnp.zeros_like(st_ref)
        stb_ref[...] = jnp.zeros_like(stb_ref)
        vbd_ref[...] = jnp.zeros_like(vbd_ref)

    z = jnp.dot(lr_ref[...].astype(BF16), w2_ref[...], preferred_element_type=F32) + gbias_ref[...]
    log_a = -(jnp.maximum(-z, 0.0) + jnp.log1p(jnp.exp(-jnp.abs(z)))) * (1.0 / GLA_TAU)
    la_ref[...] = jnp.concatenate(_split3(log_a), axis=1)

    row = lax.broadcasted_iota(jnp.int32, (C, GLA_KEY_WIDTH), 0)
    col = lax.broadcasted_iota(jnp.int32, (C, GLA_KEY_WIDTH), 1) & (C - 1)
    causal = (col >= row) if reverse else (col <= row)
    hms = [hm_ref[h:h + 1, :] for h in range(GLA_HEADS)]

    for i in range(n_chunks):
        c = (n_chunks - 1 - i) if reverse else i
        rows = slice(c * C, c * C + C)
        q = qk_ref[rows, 0:GLA_KEY_WIDTH].astype(F32) * (GLA_DK ** -0.5)
        k = qk_ref[rows, GLA_KEY_WIDTH:2 * GLA_KEY_WIDTH].astype(F32)
        v = v_ref[rows, :]
        cs = jnp.dot(tri_ref[...], la_ref[rows, :], preferred_element_type=F32)
        b = cs[:, 0:256] + cs[:, 256:512] + cs[:, 512:768]
        b_last = b[0:1, :] if reverse else b[C - 1:C, :]
        q_dec = (q * jnp.exp(b)).astype(BF16)
        k_dec = (k * jnp.exp(-b)).astype(BF16)
        k_rem = (k * jnp.exp(b_last - b)).astype(BF16)
        decay = jnp.exp(b_last)
        kbd = jnp.concatenate([k_dec * hms[h] for h in range(GLA_HEADS)], axis=0)
        for h in range(GLA_HEADS):
            vbd_ref[i, C * h:C * h + C, GLA_DV * h:GLA_DV * h + GLA_DV] = v[:, GLA_DV * h:GLA_DV * h + GLA_DV]
        scores = lax.dot_general(q_dec, kbd, (((1,), (1,)), ((), ())), preferred_element_type=F32)
        scores = jnp.where(causal, scores, 0.0).astype(BF16)
        o = jnp.dot(scores, vbd_ref[i], preferred_element_type=F32)
        o = o + lax.dot_general(q_dec, stb_ref[...], (((1,), (1,)), ((), ())), preferred_element_type=F32)
        for h in range(GLA_HEADS):
            lb = 128 * (h // 2)
            kr = (k_rem * hms[h])[:, lb:lb + 128]
            kv_t = lax.dot_general(v[:, GLA_DV * h:GLA_DV * h + GLA_DV], kr, (((0,), (0,)), ((), ())),
                                   preferred_element_type=F32)
            blk = (slice(GLA_DV * h, GLA_DV * h + GLA_DV), slice(lb, lb + 128))
            new = decay[:, lb:lb + 128] * st_ref[blk] + kv_t
            st_ref[blk] = new
            stb_ref[blk] = new.astype(BF16)
        if final:
            o = o + obwd_ref[rows, :]
            parts = []
            for h in range(GLA_HEADS):
                oh = o[:, GLA_DV * h:GLA_DV * h + GLA_DV]
                ms = jnp.mean(oh * oh, axis=-1, keepdims=True)
                parts.append(oh * lax.rsqrt(ms + RMS_EPS))
            o = jnp.concatenate(parts, axis=1) * gnorm_ref[...]
            g = gate_ref[rows, :].astype(F32)
            out_ref[rows, :] = (o * (g / (1.0 + jnp.exp(-g)))).astype(out_ref.dtype)
        else:
            out_ref[rows, :] = o


def _gla_pass(qk, v, lr, w2p, gbias, tri, hmask, extra, *, reverse, final, bsz, name):
    tile = TOKEN_TILE
    n_tok = qk.shape[0]
    n_tiles = n_tok // bsz // tile
    if reverse:
        rmap = lambda b, i: (b * n_tiles + (n_tiles - 1 - i), 0)
    else:
        rmap = lambda b, i: (b * n_tiles + i, 0)
    cmap = lambda b, i: (0, 0)
    vw = GLA_VAL_WIDTH
    in_specs = [pl.BlockSpec((tile, 2 * GLA_KEY_WIDTH), rmap),
                pl.BlockSpec((tile, vw), rmap),
                pl.BlockSpec((tile, LR_PAD), rmap),
                pl.BlockSpec(w2p.shape, cmap), pl.BlockSpec(gbias.shape, cmap),
                pl.BlockSpec(tri.shape, cmap), pl.BlockSpec(hmask.shape, cmap)]
    args = [qk, v, lr, w2p, gbias, tri, hmask]
    if final:
        gate, obwd, gnorm = extra
        in_specs += [pl.BlockSpec((tile, vw), rmap), pl.BlockSpec((tile, vw), rmap),
                     pl.BlockSpec(gnorm.shape, cmap)]
        args += [gate, obwd, gnorm]
    return pl.pallas_call(
        functools.partial(_gla_body, reverse=reverse, final=final, tile=tile),
        grid=(bsz, n_tiles),
        in_specs=in_specs,
        out_specs=pl.BlockSpec((tile, vw), rmap),
        out_shape=jax.ShapeDtypeStruct((n_tok, vw), BF16 if final else F32),
        scratch_shapes=[pltpu.VMEM((GLA_VAL_WIDTH, GLA_KEY_WIDTH), F32),
                        pltpu.VMEM((GLA_VAL_WIDTH, GLA_KEY_WIDTH), BF16),
                        pltpu.VMEM((tile // GLA_CHUNK, GLA_HEADS * GLA_CHUNK, GLA_VAL_WIDTH), BF16),
                        pltpu.VMEM((tile, 3 * GLA_KEY_WIDTH), BF16)],
        compiler_params=_params(2),
        name=name,
    )(*args)


def _final_body(x_ref, o0_ref, l0_ref, o1_ref, l1_ref, o2_ref, l2_ref, ga_ref, yg_ref, gates_ref,
                wa_ref, wg_ref, wo_ref, lng_ref, lnb_ref, out_ref, il_ref, *, tm):
    def interleave(src_ref, dil, slot):
        for r in range(dil):
            val = src_ref[r].astype(F32)
            for half in range(2):
                il_ref[slot + half, pl.ds(r, tm // dil, stride=dil), :] = val[:, 128 * half:128 * half + 128]
        return jnp.concatenate([il_ref[slot], il_ref[slot + 1]], axis=1)

    so1, sl1 = interleave(o1_ref, 4, 0), interleave(l1_ref, 4, 2)
    so2, sl2 = interleave(o2_ref, 16, 4), interleave(l2_ref, 16, 6)
    l0, l1, l2 = l0_ref[...], sl1, sl2
    mx = jnp.maximum(jnp.maximum(l0, l1), l2)
    e0, e1, e2 = jnp.exp(l0 - mx), jnp.exp(l1 - mx), jnp.exp(l2 - mx)
    inv = 1.0 / (e0 + e1 + e2)
    y_att = jnp.concatenate([o0_ref[...].astype(F32) * (e0 * inv),
                             so1 * (e1 * inv),
                             so2 * (e2 * inv)], axis=1)
    ga = ga_ref[...].astype(F32)
    y_att = (y_att * (ga / (1.0 + jnp.exp(-ga)))).astype(BF16)
    ya = jnp.dot(y_att, wa_ref[...], preferred_element_type=F32)
    yg = jnp.dot(yg_ref[...], wg_ref[...], preferred_element_type=F32)
    g_att = gates_ref[:, 0:D_MODEL].astype(F32)
    g_gla = gates_ref[:, D_MODEL:2 * D_MODEL].astype(F32)
    merged = ya / (1.0 + jnp.exp(-g_att)) + yg / (1.0 + jnp.exp(-g_gla))
    h = DEEPNORM_ALPHA * x_ref[...] + jnp.dot(merged.astype(BF16), wo_ref[...], preferred_element_type=F32)
    mu = jnp.mean(h, axis=-1, keepdims=True)
    hc = h - mu
    var = jnp.mean(hc * hc, axis=-1, keepdims=True)
    out_ref[...] = hc * lax.rsqrt(var + LN_EPS) * lng_ref[...] + lnb_ref[...]


def _finalize(x2, o0, l0, o1, l1, o2, l2, ga, y_gla, gates, wa, wg, wo, lng, lnb, *, bsz):
    n_tok, dm = x2.shape
    tm = TOKEN_TILE
    nt = n_tok // bsz // tm
    gw = GROUP_WIDTH
    rmap = lambda b, i: (b * nt + i, 0)

    def rows(arr):
        return arr, pl.BlockSpec((tm, arr.shape[-1]), rmap)

    def slabs(arr):
        dil = arr.shape[1]
        return arr, pl.BlockSpec((None, dil, tm // dil, arr.shape[-1]), lambda b, i: (b, 0, i, 0))

    def const(arr):
        return arr, pl.BlockSpec(arr.shape, lambda b, i: (0,) * arr.ndim)

    pairs = [rows(x2), rows(o0.reshape(n_tok, gw)), rows(l0.reshape(n_tok, gw)), slabs(o1), slabs(l1),
             slabs(o2), slabs(l2), rows(ga), rows(y_gla), rows(gates),
             const(wa), const(wg), const(wo), const(lng), const(lnb)]
    return pl.pallas_call(
        functools.partial(_final_body, tm=tm),
        grid=(bsz, nt),
        in_specs=[s for _, s in pairs],
        out_specs=pl.BlockSpec((tm, dm), rmap),
        out_shape=jax.ShapeDtypeStruct((n_tok, dm), F32),
        scratch_shapes=[pltpu.VMEM((8, tm, 128), F32)],
        compiler_params=_params(2),
        name="merge_out",
    )(*[a for a, _ in pairs])


def _layer(x, w_in, gate_w2, gate_b, gla_norm_g, rel_bias, w_att_out, w_gla_out, w_out, ln_g, ln_b):
    bsz, seq, dm = x.shape
    gw = GROUP_WIDTH
    n_tok = bsz * seq

    def qkv_cols(g):
        return [w_in[:, c + gw * g:c + gw * g + gw] for c in (_C_QA, _C_KA, _C_VA)]

    lr_cols = jnp.pad(w_in[:, _C_LR:_C_GATES], ((0, 0), (0, LR_PAD - 2 * GLA_GATE_RANK)))
    w_nat = jnp.concatenate(qkv_cols(0) + [w_in[:, _C_QB:_C_VB], w_in[:, _C_VB:_C_GB], w_in[:, _C_GB:_C_LR],
                                           lr_cols, w_in[:, _C_GA:_C_QB], w_in[:, _C_GATES:_C_END]],
                            axis=1).astype(BF16)
    w_d4 = jnp.concatenate(qkv_cols(1), axis=1).astype(BF16)
    w_d16 = jnp.concatenate(qkv_cols(2), axis=1).astype(BF16)
    qkv0, qk, vv, gb, lr, ga, gates, qkv1, qkv2 = _project(x, w_nat, w_d4, w_d16)

    lane = np.arange(128)
    hmask_att = jnp.asarray(np.stack([lane < 64, lane >= 64]).astype(np.float32) * ATT_HEAD_DIM ** -0.5, BF16)
    bias_all = _attn_bias(rel_bias)
    outs = []
    for g, (qkv, tq) in enumerate(((qkv0.reshape(bsz, 1, seq, ATT_WIDTH), 1024), (qkv1, 1024), (qkv2, 512))):
        outs.append(_attention(qkv, bias_all, g, hmask_att, tq, "attn_d%d" % ATT_GROUPS[g][1]))
    (o0, l0), (o1, l1), (o2, l2) = outs

    lane = np.arange(GLA_KEY_WIDTH)
    hmask_gla = jnp.asarray((lane[None, :] // GLA_DK == np.arange(GLA_HEADS)[:, None]).astype(np.float32), BF16)
    ii = np.arange(GLA_CHUNK)
    tri_f = jnp.asarray((ii[None, :] <= ii[:, None]).astype(np.float32), BF16)
    tri_b = jnp.asarray((ii[None, :] >= ii[:, None]).astype(np.float32), BF16)

    def w2_padded(direction):
        lo = GLA_GATE_RANK * direction
        return jnp.pad(gate_w2[direction], ((lo, LR_PAD - lo - GLA_GATE_RANK), (0, 0))).astype(BF16)

    o_bwd = _gla_pass(qk, vv, lr, w2_padded(1), gate_b[1][None, :].astype(F32), tri_b, hmask_gla, None,
                      reverse=True, final=False, bsz=bsz, name="gla_bwd")
    y_gla = _gla_pass(qk, vv, lr, w2_padded(0), gate_b[0][None, :].astype(F32), tri_f, hmask_gla,
                      (gb, o_bwd, gla_norm_g[None, :].astype(F32)),
                      reverse=False, final=True, bsz=bsz, name="gla_fwd")

    out = _finalize(x.reshape(n_tok, dm), o0, l0, o1, l1, o2, l2, ga, y_gla, gates,
                    w_att_out.astype(BF16), w_gla_out.astype(BF16), w_out.astype(BF16),
                    ln_g[None, :].astype(F32), ln_b[None, :].astype(F32), bsz=bsz)
    return out.reshape(bsz, seq, dm)


def kernel(x, w_in, gla_gate_w2, gla_gate_b, gla_norm_g, rel_bias, w_att_out, w_gla_out, w_out, ln_g, ln_b):
    for layer in range(DEPTH):
        x = _layer(x, w_in[layer], gla_gate_w2[layer], gla_gate_b[layer], gla_norm_g[layer], rel_bias,
                   w_att_out[layer], w_gla_out[layer], w_out[layer], ln_g[layer], ln_b[layer])
    return x
```

```python
import functools

import numpy as np
import jax
import jax.numpy as jnp
from jax import lax
from jax.experimental import pallas as pl
from jax.experimental.pallas import tpu as pltpu

F32 = jnp.float32
BF16 = jnp.bfloat16

D_MODEL = 1024
ATT_GROUPS = ((128, 1), (512, 4), (2048, 16))
HEADS_PER_GROUP = 4
ATT_HEAD_DIM = 64
GROUP_WIDTH = HEADS_PER_GROUP * ATT_HEAD_DIM
ATT_WIDTH = GROUP_WIDTH * len(ATT_GROUPS)
ATT_HALF = 64
NEG_INF = -1e30
GLA_HEADS = 4
GLA_DK = 64
GLA_DV = 128
GLA_KEY_WIDTH = GLA_HEADS * GLA_DK
GLA_VAL_WIDTH = GLA_HEADS * GLA_DV
GLA_GATE_RANK = 16
GLA_TAU = 16.0
GLA_CHUNK = 64
REL_BUCKETS = 32
REL_MAX_DISTANCE = 1024
DEPTH = 1
DEEPNORM_ALPHA = (2.0 * DEPTH) ** 0.25
LN_EPS = 1e-5
RMS_EPS = 1e-6

_C_QA, _C_KA, _C_VA, _C_GA = 0, 768, 1536, 2304
_C_QB, _C_KB, _C_VB, _C_GB = 3072, 3328, 3584, 4096
_C_LR, _C_GATES, _C_END = 4608, 4640, 6688
LR_PAD = 128

ATT_QBLK = 128
ATT_KWIN = 256
ATT_UNROLL = 2
BIAS_SPAN = 512
TOKEN_TILE = 512

VMEM_LIMIT = 56 * 1024 * 1024


def _params(n_axes):
    return pltpu.CompilerParams(dimension_semantics=("arbitrary",) * n_axes,
                                vmem_limit_bytes=VMEM_LIMIT)


def _split3(a):
    a0 = a.astype(BF16)
    r1 = a - a0.astype(F32)
    a1 = r1.astype(BF16)
    a2 = (r1 - a1.astype(F32)).astype(BF16)
    return a0, a1, a2


_NAT_WIDTHS = (ATT_WIDTH, 512, 512, 512, LR_PAD, ATT_WIDTH, 2 * D_MODEL)


def _proj_body(*refs, tm):
    n_xb = D_MODEL // 128
    x_refs, (wn_ref, w4_ref, w16_ref), o_refs = refs[:n_xb], refs[n_xb:n_xb + 3], refs[n_xb + 3:]
    nat_refs, qkv1_ref, qkv2_ref = o_refs[:-2], o_refs[-2], o_refs[-1]
    x = jnp.concatenate([xr[...].astype(BF16) for xr in x_refs], axis=1)
    off = 0
    for o_ref, n in zip(nat_refs, _NAT_WIDTHS):
        for c in range(0, n, 512):
            w = min(512, n - c)
            acc = jnp.dot(x, wn_ref[:, off + c:off + c + w], preferred_element_type=F32)
            o_ref[:, c:c + w] = acc.astype(o_ref.dtype)
        off += n
    for dil, w_ref, o_ref in ((4, w4_ref, qkv1_ref), (16, w16_ref, qkv2_ref)):
        n = tm // dil
        xs = jnp.concatenate(
            [jnp.concatenate([xr[pl.ds(r, n, stride=dil), :].astype(BF16) for xr in x_refs], axis=1)
             for r in range(dil)], axis=0)
        for c in range(0, ATT_WIDTH, GROUP_WIDTH):
            acc = jnp.dot(xs, w_ref[:, c:c + GROUP_WIDTH], preferred_element_type=F32).astype(o_ref.dtype)
            for r in range(dil):
                o_ref[r, :, c:c + GROUP_WIDTH] = acc[r * n:(r + 1) * n]


def _project(x, w_nat, w_d4, w_d16):
    bsz, seq, dm = x.shape
    tm = TOKEN_TILE
    nt = seq // tm
    n_tok = bsz * seq
    const = lambda w: pl.BlockSpec(w.shape, lambda b, i: (0, 0), pipeline_mode=pl.Buffered(1))
    nat_dtypes = (BF16, BF16, BF16, BF16, F32, BF16, BF16)
    out_specs = [pl.BlockSpec((tm, n), lambda b, i: (b * nt + i, 0)) for n in _NAT_WIDTHS]
    out_shape = [jax.ShapeDtypeStruct((n_tok, n), dt) for n, dt in zip(_NAT_WIDTHS, nat_dtypes)]
    for dil in (4, 16):
        out_specs.append(pl.BlockSpec((None, dil, tm // dil, ATT_WIDTH), lambda b, i: (b, 0, i, 0)))
        out_shape.append(jax.ShapeDtypeStruct((bsz, dil, seq // dil, ATT_WIDTH), BF16))
    x2 = x.reshape(n_tok, dm)
    x_specs = [pl.BlockSpec((tm, 128), lambda b, i, j=j: (b * nt + i, j)) for j in range(dm // 128)]
    return pl.pallas_call(
        functools.partial(_proj_body, tm=tm),
        grid=(bsz, nt),
        in_specs=x_specs + [const(w_nat), const(w_d4), const(w_d16)],
        out_specs=out_specs,
        out_shape=out_shape,
        compiler_params=_params(2),
        name="proj",
    )(*([x2] * len(x_specs)), w_nat, w_d4, w_d16)


def _t5_bucket(rel):
    half = REL_BUCKETS // 2
    max_exact = half // 2
    ret = (rel > 0).astype(np.int32) * half
    n = np.abs(rel)
    large = max_exact + (np.log(np.maximum(n, 1) / max_exact)
                         / np.log(REL_MAX_DISTANCE / max_exact) * (half - max_exact)).astype(np.int32)
    large = np.minimum(large, half - 1)
    return ret + np.where(n < max_exact, n, large)


def _bucket_onehot():
    m = np.arange(BIAS_SPAN)
    dl = np.where(m < BIAS_SPAN // 2, m, m - BIAS_SPAN)
    out = np.zeros((len(ATT_GROUPS), REL_BUCKETS, BIAS_SPAN), np.float32)
    for g, (_, dilation) in enumerate(ATT_GROUPS):
        out[g, _t5_bucket(dl * dilation), m] = 1.0
    return out


def _bias_body(rb_ref, oh_ref, out_ref):
    oh = oh_ref[...]
    table = sum(jnp.dot(t, oh, preferred_element_type=F32) for t in _split3(rb_ref[...]))
    m = lax.broadcasted_iota(jnp.int32, table.shape, 1)
    table = jnp.where((m <= ATT_HALF) | (m >= BIAS_SPAN - ATT_HALF), table, NEG_INF)
    for h in range(HEADS_PER_GROUP):
        rows = jnp.broadcast_to(table[h:h + 1, :], (ATT_QBLK, BIAS_SPAN))
        for kind in range(3):
            tile = pltpu.roll(rows, kind * ATT_HALF, 1, stride=1, stride_axis=0)
            out_ref[kind, h // 2, ATT_QBLK * (h % 2):ATT_QBLK * (h % 2 + 1), :] = tile[:, :ATT_KWIN]


def _attn_bias(rel_bias):
    n_g = len(ATT_GROUPS)
    rb = jnp.transpose(rel_bias.astype(F32)).reshape(n_g, HEADS_PER_GROUP, REL_BUCKETS)
    rb = jnp.pad(rb, ((0, 0), (0, 8 - HEADS_PER_GROUP), (0, 0)))
    onehot = jnp.asarray(_bucket_onehot(), BF16)
    return pl.pallas_call(
        _bias_body,
        grid=(n_g,),
        in_specs=[pl.BlockSpec((None, 8, REL_BUCKETS), lambda g: (g, 0, 0)),
                  pl.BlockSpec((None, REL_BUCKETS, BIAS_SPAN), lambda g: (g, 0, 0))],
        out_specs=pl.BlockSpec((None, 3, 2, 2 * ATT_QBLK, ATT_KWIN), lambda g: (g, 0, 0, 0, 0)),
        out_shape=jax.ShapeDtypeStruct((n_g, 3, 2, 2 * ATT_QBLK, ATT_KWIN), F32),
        compiler_params=_params(1),
        name="attn_bias",
    )(rb, onehot)


def _attn_body(q_ref, k_ref, v_ref, bias_ref, hm_ref, o_ref, lse_ref, *, sub, tq, nb):
    qt = pl.program_id(2)
    lo = lax.broadcasted_iota(jnp.int32, (ATT_QBLK, 128), 1) < ATT_HEAD_DIM
    hm0 = hm_ref[0:1, :]
    hm1 = hm_ref[1:2, :]

    def step(jj, carry):
        work = []
        for u in range(nb):
            j = jj * nb + u
            q0 = qt * tq + j * ATT_QBLK
            ws = pl.multiple_of(jnp.clip(q0 - ATT_HALF, 0, sub - ATT_KWIN), ATT_HALF)
            kind = jnp.where(q0 == 0, 0, jnp.where(q0 == sub - ATT_QBLK, 2, 1))
            r0 = pl.multiple_of(j * ATT_QBLK, ATT_QBLK)
            for p in range(2):
                work.append((r0, ws, kind, p, slice(128 * p, 128 * p + 128)))
        scores = []
        for r0, ws, kind, p, cols in work:
            q = q_ref[pl.ds(r0, ATT_QBLK), cols]
            qs = jnp.concatenate([q * hm0, q * hm1], axis=0)
            k = k_ref[pl.ds(ws, ATT_KWIN), cols]
            s = lax.dot_general(qs, k, (((1,), (1,)), ((), ())), preferred_element_type=F32)
            scores.append(s + bias_ref[kind, p])
        probs = []
        for s in scores:
            m = jnp.max(s, axis=-1, keepdims=True)
            e = jnp.exp(s - m)
            probs.append((e.astype(BF16), m, jnp.sum(e, axis=-1, keepdims=True)))
        outs = []
        for (r0, ws, kind, p, cols), (e, m, den) in zip(work, probs):
            v = v_ref[pl.ds(ws, ATT_KWIN), cols]
            outs.append(jnp.dot(e, v, preferred_element_type=F32))
        for (r0, ws, kind, p, cols), (e, m, den), pv in zip(work, probs, outs):
            pv = pv / den
            lse = m + jnp.log(den)
            o_ref[pl.ds(r0, ATT_QBLK), cols] = jnp.where(lo, pv[:ATT_QBLK], pv[ATT_QBLK:]).astype(o_ref.dtype)
            lse_ref[pl.ds(r0, ATT_QBLK), cols] = jnp.where(lo, lse[:ATT_QBLK], lse[ATT_QBLK:])
        return carry

    lax.fori_loop(0, tq // (ATT_QBLK * nb), step, 0)


def _attention(qkv, bias_all, g, hmask, tq, name):
    bsz, dil, sub, _ = qkv.shape
    gw = GROUP_WIDTH
    return pl.pallas_call(
        functools.partial(_attn_body, sub=sub, tq=tq, nb=ATT_UNROLL),
        grid=(bsz, dil, sub // tq),
        in_specs=[pl.BlockSpec((None, None, tq, gw), lambda b, r, i: (b, r, i, 0)),
                  pl.BlockSpec((None, None, sub, gw), lambda b, r, i: (b, r, 0, 1)),
                  pl.BlockSpec((None, None, sub, gw), lambda b, r, i: (b, r, 0, 2)),
                  pl.BlockSpec((None,) + bias_all.shape[1:], lambda b, r, i: (g, 0, 0, 0, 0)),
                  pl.BlockSpec(hmask.shape, lambda b, r, i: (0, 0))],
        out_specs=[pl.BlockSpec((None, None, tq, gw), lambda b, r, i: (b, r, i, 0)),
                   pl.BlockSpec((None, None, tq, gw), lambda b, r, i: (b, r, i, 0))],
        out_shape=[jax.ShapeDtypeStruct((bsz, dil, sub, gw), BF16),
                   jax.ShapeDtypeStruct((bsz, dil, sub, gw), F32)],
        compiler_params=_params(3),
        name=name,
    )(qkv, qkv, qkv, bias_all, hmask)


def _gla_body(*refs, reverse, final, tile):
    if final:
        (qk_ref, v_ref, lr_ref, w2_ref, gbias_ref, tri_ref, pm_ref, gate_ref, obwd_ref, gnorm_ref,
         out_ref, st_ref, wb_ref, la_ref, kv_ref, dcol_ref, b_ref, qd_ref, kd_ref, kr_ref, sc_ref, o_ref) = refs
    else:
        (qk_ref, v_ref, lr_ref, w2_ref, gbias_ref, tri_ref, pm_ref,
         out_ref, st_ref, wb_ref, la_ref, kv_ref, dcol_ref, b_ref, qd_ref, kd_ref, kr_ref, sc_ref) = refs
    C = GLA_CHUNK
    n_chunks = tile // C
    n_pairs = GLA_HEADS // 2
    pw = 2 * GLA_DK
    pv = 2 * GLA_DV

    @pl.when(pl.program_id(1) == 0)
    def _():
        st_ref[...] = jnp.zeros_like(st_ref)
        wb_ref[...] = jnp.zeros_like(wb_ref)

    z = jnp.dot(lr_ref[...].astype(BF16), w2_ref[...], preferred_element_type=F32) + gbias_ref[...]
    log_a = (jnp.minimum(z, 0.0) - jnp.log(1.0 + jnp.exp(-jnp.abs(z)))) * (1.0 / GLA_TAU)
    la_hi = log_a.astype(BF16)
    la_ref[...] = jnp.concatenate([la_hi, (log_a - la_hi.astype(F32)).astype(BF16)], axis=1)

    row = lax.broadcasted_iota(jnp.int32, (C, pw), 0)
    col = lax.broadcasted_iota(jnp.int32, (C, pw), 1) & (C - 1)
    causal = (col >= row) if reverse else (col <= row)
    same_head = ((lax.broadcasted_iota(jnp.int32, (pw, pv), 0) >= GLA_DK)
                 == (lax.broadcasted_iota(jnp.int32, (pw, pv), 1) >= GLA_DV))
    pm0, pm1 = pm_ref[0:1, :], pm_ref[1:2, :]
    order = [(n_chunks - 1 - i) if reverse else i for i in range(n_chunks)]

    chunk_rows = [slice(c * C, c * C + C) for c in order]
    pair_lanes = [slice(pw * p, pw * p + pw) for p in range(n_pairs)]

    for rows in chunk_rows:
        cs = jnp.dot(tri_ref[...], la_ref[rows, :], preferred_element_type=F32)
        b_ref[rows, :] = cs[:, 0:GLA_KEY_WIDTH] + cs[:, GLA_KEY_WIDTH:2 * GLA_KEY_WIDTH]

    for i, rows in enumerate(chunk_rows):
        b = b_ref[rows, :]
        b_last = b[0:1, :] if reverse else b[C - 1:C, :]
        decay = jnp.exp(b_last)
        inv_b = jnp.exp(-b)
        k = qk_ref[rows, GLA_KEY_WIDTH:2 * GLA_KEY_WIDTH].astype(F32)
        qd_ref[rows, :] = (qk_ref[rows, 0:GLA_KEY_WIDTH].astype(F32) * jnp.exp(b)).astype(BF16)
        kd_ref[rows, :] = (k * inv_b).astype(BF16)
        kr_ref[rows, :] = (k * (inv_b * decay)).astype(BF16)
        for p, lanes in enumerate(pair_lanes):
            dcol_ref[i, p] = jnp.transpose(jnp.broadcast_to(decay[:, lanes], (pw, pw)))
            for hh in range(2):
                h = 2 * p + hh
                wb_ref[i, p, C * hh:C * hh + C, GLA_DV * hh:GLA_DV * hh + GLA_DV] = (
                    v_ref[rows, GLA_DV * h:GLA_DV * h + GLA_DV])

    for i, rows in enumerate(chunk_rows):
        for p, lanes in enumerate(pair_lanes):
            kd = kd_ref[rows, lanes]
            kbd = jnp.concatenate([kd * pm0, kd * pm1], axis=0)
            s = lax.dot_general(qd_ref[rows, lanes], kbd, (((1,), (1,)), ((), ())), preferred_element_type=F32)
            sc_ref[rows, lanes] = jnp.where(causal, s, 0.0).astype(BF16)
    for i, rows in enumerate(chunk_rows):
        for p, lanes in enumerate(pair_lanes):
            kv = lax.dot_general(kr_ref[rows, lanes], v_ref[rows, pv * p:pv * p + pv], (((0,), (0,)), ((), ())),
                                 preferred_element_type=F32)
            kv_ref[i, p] = jnp.where(same_head, kv, 0.0)

    for i in range(n_chunks):
        for p in range(n_pairs):
            s = st_ref[p]
            wb_ref[i, p, pw:2 * pw, :] = s.astype(BF16)
            d = dcol_ref[i, p]
            st_ref[p] = jnp.concatenate([d, d], axis=1) * s + kv_ref[i, p]

    acc_ref = o_ref if final else out_ref
    for i, rows in enumerate(chunk_rows):
        for p, lanes in enumerate(pair_lanes):
            lhs = jnp.concatenate([sc_ref[rows, lanes], qd_ref[rows, lanes]], axis=1)
            acc_ref[rows, pv * p:pv * p + pv] = jnp.dot(lhs, wb_ref[i, p], preferred_element_type=F32)

    if final:
        for rows in chunk_rows:
            o = o_ref[rows, :] + obwd_ref[rows, :]
            parts = []
            for h in range(GLA_HEADS):
                oh = o[:, GLA_DV * h:GLA_DV * h + GLA_DV]
                ms = jnp.mean(oh * oh, axis=-1, keepdims=True)
                parts.append(oh * lax.rsqrt(ms + RMS_EPS))
            o = jnp.concatenate(parts, axis=1) * gnorm_ref[...]
            hg = 0.5 * gate_ref[rows, :].astype(F32)
            out_ref[rows, :] = (o * (hg * (jnp.tanh(hg) + 1.0))).astype(out_ref.dtype)


def _gla_pass(qk, v, lr, w2p, gbias, tri, hmask, extra, *, reverse, final, bsz, name):
    tile = TOKEN_TILE
    n_tok = qk.shape[0]
    n_tiles = n_tok // bsz // tile
    n_chunks = tile // GLA_CHUNK
    n_pairs, pw, pv = GLA_HEADS // 2, 2 * GLA_DK, 2 * GLA_DV
    if reverse:
        rmap = lambda b, i: (b * n_tiles + (n_tiles - 1 - i), 0)
    else:
        rmap = lambda b, i: (b * n_tiles + i, 0)
    cmap = lambda b, i: (0, 0)
    vw = GLA_VAL_WIDTH
    in_specs = [pl.BlockSpec((tile, 2 * GLA_KEY_WIDTH), rmap),
                pl.BlockSpec((tile, vw), rmap),
                pl.BlockSpec((tile, LR_PAD), rmap),
                pl.BlockSpec(w2p.shape, cmap), pl.BlockSpec(gbias.shape, cmap),
                pl.BlockSpec(tri.shape, cmap), pl.BlockSpec(hmask.shape, cmap)]
    args = [qk, v, lr, w2p, gbias, tri, hmask]
    if final:
        gate, obwd, gnorm = extra
        in_specs += [pl.BlockSpec((tile, vw), rmap), pl.BlockSpec((tile, vw), rmap),
                     pl.BlockSpec(gnorm.shape, cmap)]
        args += [gate, obwd, gnorm]
    scratch = [pltpu.VMEM((n_pairs, pw, pv), F32),
               pltpu.VMEM((n_chunks, n_pairs, 2 * pw, pv), BF16),
               pltpu.VMEM((tile, 2 * GLA_KEY_WIDTH), BF16),
               pltpu.VMEM((n_chunks, n_pairs, pw, pv), F32),
               pltpu.VMEM((n_chunks, n_pairs, pw, pw), F32),
               pltpu.VMEM((tile, GLA_KEY_WIDTH), F32),
               pltpu.VMEM((tile, GLA_KEY_WIDTH), BF16),
               pltpu.VMEM((tile, GLA_KEY_WIDTH), BF16),
               pltpu.VMEM((tile, GLA_KEY_WIDTH), BF16),
               pltpu.VMEM((tile, GLA_KEY_WIDTH), BF16)]
    if final:
        scratch.append(pltpu.VMEM((tile, vw), F32))
    return pl.pallas_call(
        functools.partial(_gla_body, reverse=reverse, final=final, tile=tile),
        grid=(bsz, n_tiles),
        in_specs=in_specs,
        out_specs=pl.BlockSpec((tile, vw), rmap),
        out_shape=jax.ShapeDtypeStruct((n_tok, vw), BF16 if final else F32),
        scratch_shapes=scratch,
        compiler_params=_params(2),
        name=name,
    )(*args)


def _final_body(x_ref, o0_ref, l0_ref, o1_ref, l1_ref, o2_ref, l2_ref, ga_ref, yg_ref, gates_ref,
                wa_ref, wg_ref, wo_ref, lng_ref, lnb_ref, out_ref, il_ref, *, tm):
    def interleave(src_ref, dil, slot):
        for r in range(dil):
            val = src_ref[r].astype(F32)
            for half in range(2):
                il_ref[slot + half, pl.ds(r, tm // dil, stride=dil), :] = val[:, 128 * half:128 * half + 128]
        return jnp.concatenate([il_ref[slot], il_ref[slot + 1]], axis=1)

    so1, sl1 = interleave(o1_ref, 4, 0), interleave(l1_ref, 4, 2)
    so2, sl2 = interleave(o2_ref, 16, 4), interleave(l2_ref, 16, 6)
    l0, l1, l2 = l0_ref[...], sl1, sl2
    mx = jnp.maximum(jnp.maximum(l0, l1), l2)
    e0, e1, e2 = jnp.exp(l0 - mx), jnp.exp(l1 - mx), jnp.exp(l2 - mx)
    inv = 1.0 / (e0 + e1 + e2)
    y_att = jnp.concatenate([o0_ref[...].astype(F32) * (e0 * inv),
                             so1 * (e1 * inv),
                             so2 * (e2 * inv)], axis=1)
    hga = 0.5 * ga_ref[...].astype(F32)
    y_att = (y_att * (hga * (jnp.tanh(hga) + 1.0))).astype(BF16)
    ya = jnp.dot(y_att, wa_ref[...], preferred_element_type=F32)
    yg = jnp.dot(yg_ref[...], wg_ref[...], preferred_element_type=F32)
    t_att = jnp.tanh(0.5 * gates_ref[:, 0:D_MODEL].astype(F32))
    t_gla = jnp.tanh(0.5 * gates_ref[:, D_MODEL:2 * D_MODEL].astype(F32))
    merged = 0.5 * (ya * (t_att + 1.0) + yg * (t_gla + 1.0))
    h = DEEPNORM_ALPHA * x_ref[...] + jnp.dot(merged.astype(BF16), wo_ref[...], preferred_element_type=F32)
    mu = jnp.mean(h, axis=-1, keepdims=True)
    hc = h - mu
    var = jnp.mean(hc * hc, axis=-1, keepdims=True)
    out_ref[...] = hc * lax.rsqrt(var + LN_EPS) * lng_ref[...] + lnb_ref[...]


def _finalize(x2, o0, l0, o1, l1, o2, l2, ga, y_gla, gates, wa, wg, wo, lng, lnb, *, bsz):
    n_tok, dm = x2.shape
    tm = TOKEN_TILE
    nt = n_tok // bsz // tm
    gw = GROUP_WIDTH
    rmap = lambda b, i: (b * nt + i, 0)

    def rows(arr):
        return arr, pl.BlockSpec((tm, arr.shape[-1]), rmap)

    def slabs(arr):
        dil = arr.shape[1]
        return arr, pl.BlockSpec((None, dil, tm // dil, arr.shape[-1]), lambda b, i: (b, 0, i, 0))

    def const(arr):
        return arr, pl.BlockSpec(arr.shape, lambda b, i: (0,) * arr.ndim)

    pairs = [rows(x2), rows(o0.reshape(n_tok, gw)), rows(l0.reshape(n_tok, gw)), slabs(o1), slabs(l1),
             slabs(o2), slabs(l2), rows(ga), rows(y_gla), rows(gates),
             const(wa), const(wg), const(wo), const(lng), const(lnb)]
    return pl.pallas_call(
        functools.partial(_final_body, tm=tm),
        grid=(bsz, nt),
        in_specs=[s for _, s in pairs],
        out_specs=pl.BlockSpec((tm, dm), rmap),
        out_shape=jax.ShapeDtypeStruct((n_tok, dm), F32),
        scratch_shapes=[pltpu.VMEM((8, tm, 128), F32)],
        compiler_params=_params(2),
        name="merge_out",
    )(*[a for a, _ in pairs])


def _layer(x, w_in, gate_w2, gate_b, gla_norm_g, rel_bias, w_att_out, w_gla_out, w_out, ln_g, ln_b):
    bsz, seq, dm = x.shape
    gw = GROUP_WIDTH
    n_tok = bsz * seq

    def qkv_cols(g):
        return [w_in[:, c + gw * g:c + gw * g + gw] for c in (_C_QA, _C_KA, _C_VA)]

    lr_cols = jnp.pad(w_in[:, _C_LR:_C_GATES], ((0, 0), (0, LR_PAD - 2 * GLA_GATE_RANK)))
    w_qk_gla = jnp.concatenate([w_in[:, _C_QB:_C_KB] * (GLA_DK ** -0.5), w_in[:, _C_KB:_C_VB]], axis=1)
    w_nat = jnp.concatenate(qkv_cols(0) + [w_qk_gla, w_in[:, _C_VB:_C_GB], w_in[:, _C_GB:_C_LR],
                                           lr_cols, w_in[:, _C_GA:_C_QB], w_in[:, _C_GATES:_C_END]],
                            axis=1).astype(BF16)
    w_d4 = jnp.concatenate(qkv_cols(1), axis=1).astype(BF16)
    w_d16 = jnp.concatenate(qkv_cols(2), axis=1).astype(BF16)
    qkv0, qk, vv, gb, lr, ga, gates, qkv1, qkv2 = _project(x, w_nat, w_d4, w_d16)

    lane = np.arange(128)
    hmask_att = jnp.asarray(np.stack([lane < 64, lane >= 64]).astype(np.float32) * ATT_HEAD_DIM ** -0.5, BF16)
    bias_all = _attn_bias(rel_bias)
    outs = []
    for g, (qkv, tq) in enumerate(((qkv0.reshape(bsz, 1, seq, ATT_WIDTH), 1024), (qkv1, 1024), (qkv2, 512))):
        outs.append(_attention(qkv, bias_all, g, hmask_att, tq, "attn_d%d" % ATT_GROUPS[g][1]))
    (o0, l0), (o1, l1), (o2, l2) = outs

    lane = np.arange(2 * GLA_DK)
    hmask_gla = jnp.asarray(np.stack([lane < GLA_DK, lane >= GLA_DK]).astype(np.float32), BF16)
    ii = np.arange(GLA_CHUNK)
    tri_f = jnp.asarray((ii[None, :] <= ii[:, None]).astype(np.float32), BF16)
    tri_b = jnp.asarray((ii[None, :] >= ii[:, None]).astype(np.float32), BF16)

    def w2_padded(direction):
        lo = GLA_GATE_RANK * direction
        return jnp.pad(gate_w2[direction], ((lo, LR_PAD - lo - GLA_GATE_RANK), (0, 0))).astype(BF16)

    o_bwd = _gla_pass(qk, vv, lr, w2_padded(1), gate_b[1][None, :].astype(F32), tri_b, hmask_gla, None,
                      reverse=True, final=False, bsz=bsz, name="gla_bwd")
    y_gla = _gla_pass(qk, vv, lr, w2_padded(0), gate_b[0][None, :].astype(F32), tri_f, hmask_gla,
                      (gb, o_bwd, gla_norm_g[None, :].astype(F32)),
                      reverse=False, final=True, bsz=bsz, name="gla_fwd")

    out = _finalize(x.reshape(n_tok, dm), o0, l0, o1, l1, o2, l2, ga, y_gla, gates,
                    w_att_out.astype(BF16), w_gla_out.astype(BF16), w_out.astype(BF16),
                    ln_g[None, :].astype(F32), ln_b[None, :].astype(F32), bsz=bsz)
    return out.reshape(bsz, seq, dm)


def kernel(x, w_in, gla_gate_w2, gla_gate_b, gla_norm_g, rel_bias, w_att_out, w_gla_out, w_out, ln_g, ln_b):
    for layer in range(DEPTH):
        x = _layer(x, w_in[layer], gla_gate_w2[layer], gla_gate_b[layer], gla_norm_g[layer], rel_bias,
                   w_att_out[layer], w_gla_out[layer], w_out[layer], ln_g[layer], ln_b[layer])
    return x
```

```python
import functools

import numpy as np
import jax
import jax.numpy as jnp
from jax import lax
from jax.experimental import pallas as pl
from jax.experimental.pallas import tpu as pltpu

F32 = jnp.float32
BF16 = jnp.bfloat16

D_MODEL = 1024
ATT_GROUPS = ((128, 1), (512, 4), (2048, 16))
HEADS_PER_GROUP = 4
ATT_HEAD_DIM = 64
GROUP_WIDTH = HEADS_PER_GROUP * ATT_HEAD_DIM
ATT_WIDTH = GROUP_WIDTH * len(ATT_GROUPS)
ATT_HALF = 64
NEG_INF = -1e30
GLA_HEADS = 4
GLA_DK = 64
GLA_DV = 128
GLA_KEY_WIDTH = GLA_HEADS * GLA_DK
GLA_VAL_WIDTH = GLA_HEADS * GLA_DV
GLA_GATE_RANK = 16
GLA_TAU = 16.0
GLA_CHUNK = 64
REL_BUCKETS = 32
REL_MAX_DISTANCE = 1024
DEPTH = 1
DEEPNORM_ALPHA = (2.0 * DEPTH) ** 0.25
LN_EPS = 1e-5
RMS_EPS = 1e-6

_C_QA, _C_KA, _C_VA, _C_GA = 0, 768, 1536, 2304
_C_QB, _C_KB, _C_VB, _C_GB = 3072, 3328, 3584, 4096
_C_LR, _C_GATES, _C_END = 4608, 4640, 6688
LR_PAD = 128

ATT_QBLK = 128
ATT_KWIN = 256
ATT_UNROLL = 2
BIAS_SPAN = 512
PROJ_TILE = 1024
GLA_TILE = 1024
MERGE_TILE = 1024
MERGE_ROW_BLOCK = 256

VMEM_LIMIT = 56 * 1024 * 1024


def _params(n_axes):
    return pltpu.CompilerParams(dimension_semantics=("arbitrary",) * n_axes,
                                vmem_limit_bytes=VMEM_LIMIT)


def _split3(a):
    a0 = a.astype(BF16)
    r1 = a - a0.astype(F32)
    a1 = r1.astype(BF16)
    a2 = (r1 - a1.astype(F32)).astype(BF16)
    return a0, a1, a2


_NAT_WIDTHS = (ATT_WIDTH, 512, 512, 512, LR_PAD, ATT_WIDTH, 2 * D_MODEL)


def _proj_body(*refs, tm):
    n_xb = D_MODEL // 128
    x_refs, (wn_ref, w4_ref, w16_ref), o_refs = refs[:n_xb], refs[n_xb:n_xb + 3], refs[n_xb + 3:]
    nat_refs, qkv1_ref, qkv2_ref = o_refs[:-2], o_refs[-2], o_refs[-1]
    x = jnp.concatenate([xr[...].astype(BF16) for xr in x_refs], axis=1)
    off = 0
    for o_ref, n in zip(nat_refs, _NAT_WIDTHS):
        for c in range(0, n, 512):
            w = min(512, n - c)
            acc = jnp.dot(x, wn_ref[:, off + c:off + c + w], preferred_element_type=F32)
            o_ref[:, c:c + w] = acc.astype(o_ref.dtype)
        off += n
    for dil, w_ref, o_ref in ((4, w4_ref, qkv1_ref), (16, w16_ref, qkv2_ref)):
        n = tm // dil
        xs = jnp.concatenate(
            [jnp.concatenate([xr[pl.ds(r, n, stride=dil), :].astype(BF16) for xr in x_refs], axis=1)
             for r in range(dil)], axis=0)
        for c in range(0, ATT_WIDTH, GROUP_WIDTH):
            acc = jnp.dot(xs, w_ref[:, c:c + GROUP_WIDTH], preferred_element_type=F32).astype(o_ref.dtype)
            for r in range(dil):
                o_ref[r, :, c:c + GROUP_WIDTH] = acc[r * n:(r + 1) * n]


def _project(x, w_nat, w_d4, w_d16):
    bsz, seq, dm = x.shape
    tm = PROJ_TILE
    nt = seq // tm
    n_tok = bsz * seq
    const = lambda w: pl.BlockSpec(w.shape, lambda b, i: (0, 0), pipeline_mode=pl.Buffered(1))
    nat_dtypes = (BF16, BF16, BF16, BF16, F32, BF16, BF16)
    out_specs = [pl.BlockSpec((tm, n), lambda b, i: (b * nt + i, 0)) for n in _NAT_WIDTHS]
    out_shape = [jax.ShapeDtypeStruct((n_tok, n), dt) for n, dt in zip(_NAT_WIDTHS, nat_dtypes)]
    for dil in (4, 16):
        out_specs.append(pl.BlockSpec((None, dil, tm // dil, ATT_WIDTH), lambda b, i: (b, 0, i, 0)))
        out_shape.append(jax.ShapeDtypeStruct((bsz, dil, seq // dil, ATT_WIDTH), BF16))
    x2 = x.reshape(n_tok, dm)
    x_specs = [pl.BlockSpec((tm, 128), lambda b, i, j=j: (b * nt + i, j)) for j in range(dm // 128)]
    return pl.pallas_call(
        functools.partial(_proj_body, tm=tm),
        grid=(bsz, nt),
        in_specs=x_specs + [const(w_nat), const(w_d4), const(w_d16)],
        out_specs=out_specs,
        out_shape=out_shape,
        compiler_params=_params(2),
        name="proj",
    )(*([x2] * len(x_specs)), w_nat, w_d4, w_d16)


def _t5_bucket(rel):
    half = REL_BUCKETS // 2
    max_exact = half // 2
    ret = (rel > 0).astype(np.int32) * half
    n = np.abs(rel)
    large = max_exact + (np.log(np.maximum(n, 1) / max_exact)
                         / np.log(REL_MAX_DISTANCE / max_exact) * (half - max_exact)).astype(np.int32)
    large = np.minimum(large, half - 1)
    return ret + np.where(n < max_exact, n, large)


def _bucket_onehot():
    m = np.arange(BIAS_SPAN)
    dl = np.where(m < BIAS_SPAN // 2, m, m - BIAS_SPAN)
    out = np.zeros((len(ATT_GROUPS), REL_BUCKETS, BIAS_SPAN), np.float32)
    for g, (_, dilation) in enumerate(ATT_GROUPS):
        out[g, _t5_bucket(dl * dilation), m] = 1.0
    return out


def _bias_body(rb_ref, oh_ref, out_ref):
    oh = oh_ref[...]
    table = sum(jnp.dot(t, oh, preferred_element_type=F32) for t in _split3(rb_ref[...]))
    m = lax.broadcasted_iota(jnp.int32, table.shape, 1)
    table = jnp.where((m <= ATT_HALF) | (m >= BIAS_SPAN - ATT_HALF), table, NEG_INF)
    for h in range(HEADS_PER_GROUP):
        rows = jnp.broadcast_to(table[h:h + 1, :], (ATT_QBLK, BIAS_SPAN))
        for kind in range(3):
            tile = pltpu.roll(rows, kind * ATT_HALF, 1, stride=1, stride_axis=0)
            out_ref[kind, h // 2, ATT_QBLK * (h % 2):ATT_QBLK * (h % 2 + 1), :] = tile[:, :ATT_KWIN]


def _attn_bias(rel_bias):
    n_g = len(ATT_GROUPS)
    rb = jnp.transpose(rel_bias.astype(F32)).reshape(n_g, HEADS_PER_GROUP, REL_BUCKETS)
    rb = jnp.pad(rb, ((0, 0), (0, 8 - HEADS_PER_GROUP), (0, 0)))
    onehot = jnp.asarray(_bucket_onehot(), BF16)
    return pl.pallas_call(
        _bias_body,
        grid=(n_g,),
        in_specs=[pl.BlockSpec((None, 8, REL_BUCKETS), lambda g: (g, 0, 0)),
                  pl.BlockSpec((None, REL_BUCKETS, BIAS_SPAN), lambda g: (g, 0, 0))],
        out_specs=pl.BlockSpec((None, 3, 2, 2 * ATT_QBLK, ATT_KWIN), lambda g: (g, 0, 0, 0, 0)),
        out_shape=jax.ShapeDtypeStruct((n_g, 3, 2, 2 * ATT_QBLK, ATT_KWIN), F32),
        compiler_params=_params(1),
        name="attn_bias",
    )(rb, onehot)


def _attn_body(qa_ref, ka_ref, va_ref, bias_ref, hm_ref, oa_ref, lsea_ref, *, sub, tq, nb):
    qt = pl.program_id(2)
    lo = lax.broadcasted_iota(jnp.int32, (ATT_QBLK, 128), 1) < ATT_HEAD_DIM
    hm0 = hm_ref[0:1, :]
    hm1 = hm_ref[1:2, :]
    for slab in range(qa_ref.shape[0]):
        _attn_slab(qa_ref.at[slab], ka_ref.at[slab], va_ref.at[slab], bias_ref, oa_ref.at[slab], lsea_ref.at[slab],
                   qt=qt, lo=lo, hm0=hm0, hm1=hm1, sub=sub, tq=tq, nb=nb)


def _attn_slab(q_ref, k_ref, v_ref, bias_ref, o_ref, lse_ref, *, qt, lo, hm0, hm1, sub, tq, nb):
    def step(jj, carry):
        work = []
        for u in range(nb):
            j = jj * nb + u
            q0 = qt * tq + j * ATT_QBLK
            ws = pl.multiple_of(jnp.clip(q0 - ATT_HALF, 0, sub - ATT_KWIN), ATT_HALF)
            kind = jnp.where(q0 == 0, 0, jnp.where(q0 == sub - ATT_QBLK, 2, 1))
            r0 = pl.multiple_of(j * ATT_QBLK, ATT_QBLK)
            for p in range(2):
                work.append((r0, ws, kind, p, slice(128 * p, 128 * p + 128)))
        scores = []
        for r0, ws, kind, p, cols in work:
            q = q_ref[pl.ds(r0, ATT_QBLK), cols]
            qs = jnp.concatenate([q * hm0, q * hm1], axis=0)
            k = k_ref[pl.ds(ws, ATT_KWIN), cols]
            s = lax.dot_general(qs, k, (((1,), (1,)), ((), ())), preferred_element_type=F32)
            scores.append(s + bias_ref[kind, p])
        probs = []
        for s in scores:
            m = jnp.max(s, axis=-1, keepdims=True)
            e = jnp.exp(s - m)
            probs.append((e.astype(BF16), m, jnp.sum(e, axis=-1, keepdims=True)))
        outs = []
        for (r0, ws, kind, p, cols), (e, m, den) in zip(work, probs):
            v = v_ref[pl.ds(ws, ATT_KWIN), cols]
            outs.append(jnp.dot(e, v, preferred_element_type=F32))
        for (r0, ws, kind, p, cols), (e, m, den), pv in zip(work, probs, outs):
            pv = pv / den
            lse = m + jnp.log(den)
            o_ref[pl.ds(r0, ATT_QBLK), cols] = jnp.where(lo, pv[:ATT_QBLK], pv[ATT_QBLK:]).astype(o_ref.dtype)
            lse_ref[pl.ds(r0, ATT_QBLK), cols] = jnp.where(lo, lse[:ATT_QBLK], lse[ATT_QBLK:])
        return carry

    lax.fori_loop(0, tq // (ATT_QBLK * nb), step, 0)


def _attention(qkv, bias_all, g, hmask, tq, slabs, name):
    bsz, dil, sub, _ = qkv.shape
    gw = GROUP_WIDTH
    return pl.pallas_call(
        functools.partial(_attn_body, sub=sub, tq=tq, nb=ATT_UNROLL),
        grid=(bsz, dil // slabs, sub // tq),
        in_specs=[pl.BlockSpec((None, slabs, tq, gw), lambda b, r, i: (b, r, i, 0)),
                  pl.BlockSpec((None, slabs, sub, gw), lambda b, r, i: (b, r, 0, 1)),
                  pl.BlockSpec((None, slabs, sub, gw), lambda b, r, i: (b, r, 0, 2)),
                  pl.BlockSpec((None,) + bias_all.shape[1:], lambda b, r, i: (g, 0, 0, 0, 0)),
                  pl.BlockSpec(hmask.shape, lambda b, r, i: (0, 0))],
        out_specs=[pl.BlockSpec((None, slabs, tq, gw), lambda b, r, i: (b, r, i, 0)),
                   pl.BlockSpec((None, slabs, tq, gw), lambda b, r, i: (b, r, i, 0))],
        out_shape=[jax.ShapeDtypeStruct((bsz, dil, sub, gw), BF16),
                   jax.ShapeDtypeStruct((bsz, dil, sub, gw), F32)],
        compiler_params=_params(3),
        name=name,
    )(qkv, qkv, qkv, bias_all, hmask)


def _gla_body(*refs, reverse, final, tile):
    if final:
        (qk_ref, v_ref, lr_ref, w2_ref, gbias_ref, tri_ref, pm_ref, gate_ref, obwd_ref, gnorm_ref,
         out_ref, st_ref, wb_ref, la_ref, kv_ref, dcol_ref, b_ref, qd_ref, kd_ref, kr_ref, sc_ref, o_ref) = refs
    else:
        (qk_ref, v_ref, lr_ref, w2_ref, gbias_ref, tri_ref, pm_ref,
         out_ref, st_ref, wb_ref, la_ref, kv_ref, dcol_ref, b_ref, qd_ref, kd_ref, kr_ref, sc_ref) = refs
    C = GLA_CHUNK
    n_chunks = tile // C
    n_pairs = GLA_HEADS // 2
    pw = 2 * GLA_DK
    pv = 2 * GLA_DV

    @pl.when(pl.program_id(1) == 0)
    def _():
        st_ref[...] = jnp.zeros_like(st_ref)
        wb_ref[...] = jnp.zeros_like(wb_ref)

    z = jnp.dot(lr_ref[...].astype(BF16), w2_ref[...], preferred_element_type=F32) + gbias_ref[...]
    log_a = (jnp.minimum(z, 0.0) - jnp.log(1.0 + jnp.exp(-jnp.abs(z)))) * (1.0 / GLA_TAU)
    la_hi = log_a.astype(BF16)
    la_ref[...] = jnp.concatenate([la_hi, (log_a - la_hi.astype(F32)).astype(BF16)], axis=1)

    row = lax.broadcasted_iota(jnp.int32, (C, pw), 0)
    col = lax.broadcasted_iota(jnp.int32, (C, pw), 1) & (C - 1)
    causal = (col >= row) if reverse else (col <= row)
    same_head = ((lax.broadcasted_iota(jnp.int32, (pw, pv), 0) >= GLA_DK)
                 == (lax.broadcasted_iota(jnp.int32, (pw, pv), 1) >= GLA_DV))
    pm0, pm1 = pm_ref[0:1, :], pm_ref[1:2, :]
    order = [(n_chunks - 1 - i) if reverse else i for i in range(n_chunks)]

    chunk_rows = [slice(c * C, c * C + C) for c in order]
    pair_lanes = [slice(pw * p, pw * p + pw) for p in range(n_pairs)]

    for rows in chunk_rows:
        cs = jnp.dot(tri_ref[...], la_ref[rows, :], preferred_element_type=F32)
        b_ref[rows, :] = cs[:, 0:GLA_KEY_WIDTH] + cs[:, GLA_KEY_WIDTH:2 * GLA_KEY_WIDTH]

    for i, rows in enumerate(chunk_rows):
        b = b_ref[rows, :]
        b_last = b[0:1, :] if reverse else b[C - 1:C, :]
        decay = jnp.exp(b_last)
        inv_b = jnp.exp(-b)
        k = qk_ref[rows, GLA_KEY_WIDTH:2 * GLA_KEY_WIDTH].astype(F32)
        qd_ref[rows, :] = (qk_ref[rows, 0:GLA_KEY_WIDTH].astype(F32) * jnp.exp(b)).astype(BF16)
        kd_ref[rows, :] = (k * inv_b).astype(BF16)
        kr_ref[rows, :] = (k * (inv_b * decay)).astype(BF16)
        for p, lanes in enumerate(pair_lanes):
            dcol_ref[i, p] = jnp.transpose(jnp.broadcast_to(decay[:, lanes], (pw, pw)))
            for hh in range(2):
                h = 2 * p + hh
                wb_ref[i, p, C * hh:C * hh + C, GLA_DV * hh:GLA_DV * hh + GLA_DV] = (
                    v_ref[rows, GLA_DV * h:GLA_DV * h + GLA_DV])

    for i, rows in enumerate(chunk_rows):
        for p, lanes in enumerate(pair_lanes):
            kd = kd_ref[rows, lanes]
            kbd = jnp.concatenate([kd * pm0, kd * pm1], axis=0)
            s = lax.dot_general(qd_ref[rows, lanes], kbd, (((1,), (1,)), ((), ())), preferred_element_type=F32)
            sc_ref[rows, lanes] = jnp.where(causal, s, 0.0).astype(BF16)
    for i, rows in enumerate(chunk_rows):
        for p, lanes in enumerate(pair_lanes):
            kv = lax.dot_general(kr_ref[rows, lanes], v_ref[rows, pv * p:pv * p + pv], (((0,), (0,)), ((), ())),
                                 preferred_element_type=F32)
            kv_ref[i, p] = jnp.where(same_head, kv, 0.0)

    for i in range(n_chunks):
        for p in range(n_pairs):
            s = st_ref[p]
            wb_ref[i, p, pw:2 * pw, :] = s.astype(BF16)
            d = dcol_ref[i, p]
            st_ref[p] = jnp.concatenate([d, d], axis=1) * s + kv_ref[i, p]

    acc_ref = o_ref if final else out_ref
    for i, rows in enumerate(chunk_rows):
        for p, lanes in enumerate(pair_lanes):
            lhs = jnp.concatenate([sc_ref[rows, lanes], qd_ref[rows, lanes]], axis=1)
            acc_ref[rows, pv * p:pv * p + pv] = jnp.dot(lhs, wb_ref[i, p], preferred_element_type=F32)

    if final:
        for rows in chunk_rows:
            o = o_ref[rows, :] + obwd_ref[rows, :]
            parts = []
            for h in range(GLA_HEADS):
                oh = o[:, GLA_DV * h:GLA_DV * h + GLA_DV]
                ms = jnp.mean(oh * oh, axis=-1, keepdims=True)
                parts.append(oh * lax.rsqrt(ms + RMS_EPS))
            o = jnp.concatenate(parts, axis=1) * gnorm_ref[...]
            hg = gate_ref[rows, :].astype(F32)
            out_ref[rows, :] = (o * (hg * (jnp.tanh(hg) + 1.0))).astype(out_ref.dtype)


def _gla_pass(qk, v, lr, w2p, gbias, tri, hmask, extra, *, reverse, final, bsz, name):
    tile = GLA_TILE
    n_tok = qk.shape[0]
    n_tiles = n_tok // bsz // tile
    n_chunks = tile // GLA_CHUNK
    n_pairs, pw, pv = GLA_HEADS // 2, 2 * GLA_DK, 2 * GLA_DV
    if reverse:
        rmap = lambda b, i: (b * n_tiles + (n_tiles - 1 - i), 0)
    else:
        rmap = lambda b, i: (b * n_tiles + i, 0)
    cmap = lambda b, i: (0, 0)
    vw = GLA_VAL_WIDTH
    in_specs = [pl.BlockSpec((tile, 2 * GLA_KEY_WIDTH), rmap),
                pl.BlockSpec((tile, vw), rmap),
                pl.BlockSpec((tile, LR_PAD), rmap),
                pl.BlockSpec(w2p.shape, cmap), pl.BlockSpec(gbias.shape, cmap),
                pl.BlockSpec(tri.shape, cmap), pl.BlockSpec(hmask.shape, cmap)]
    args = [qk, v, lr, w2p, gbias, tri, hmask]
    if final:
        gate, obwd, gnorm = extra
        in_specs += [pl.BlockSpec((tile, vw), rmap), pl.BlockSpec((tile, vw), rmap),
                     pl.BlockSpec(gnorm.shape, cmap)]
        args += [gate, obwd, gnorm]
    scratch = [pltpu.VMEM((n_pairs, pw, pv), F32),
               pltpu.VMEM((n_chunks, n_pairs, 2 * pw, pv), BF16),
               pltpu.VMEM((tile, 2 * GLA_KEY_WIDTH), BF16),
               pltpu.VMEM((n_chunks, n_pairs, pw, pv), F32),
               pltpu.VMEM((n_chunks, n_pairs, pw, pw), F32),
               pltpu.VMEM((tile, GLA_KEY_WIDTH), F32),
               pltpu.VMEM((tile, GLA_KEY_WIDTH), BF16),
               pltpu.VMEM((tile, GLA_KEY_WIDTH), BF16),
               pltpu.VMEM((tile, GLA_KEY_WIDTH), BF16),
               pltpu.VMEM((tile, GLA_KEY_WIDTH), BF16)]
    if final:
        scratch.append(pltpu.VMEM((tile, vw), F32))
    return pl.pallas_call(
        functools.partial(_gla_body, reverse=reverse, final=final, tile=tile),
        grid=(bsz, n_tiles),
        in_specs=in_specs,
        out_specs=pl.BlockSpec((tile, vw), rmap),
        out_shape=jax.ShapeDtypeStruct((n_tok, vw), BF16 if final else F32),
        scratch_shapes=scratch,
        compiler_params=_params(2),
        name=name,
    )(*args)


def _final_body(x_ref, o0_ref, l0_ref, o1_ref, l1_ref, o2_ref, l2_ref, ga_ref, yg_ref, gates_ref,
                wa_ref, wg_ref, wo_ref, lng_ref, lnb_ref, out_ref, il_ref, *, tm, rb):
    for src_ref, dil, slot in ((o1_ref, 4, 0), (l1_ref, 4, 2), (o2_ref, 16, 4), (l2_ref, 16, 6)):
        for r in range(dil):
            val = src_ref[r].astype(F32)
            for half in range(2):
                il_ref[slot + half, pl.ds(r, tm // dil, stride=dil), :] = val[:, 128 * half:128 * half + 128]

    def attention_part(rows):
        def il(slot):
            return jnp.concatenate([il_ref[slot, rows, :], il_ref[slot + 1, rows, :]], axis=1)
        l0, l1, l2 = l0_ref[rows, :], il(2), il(6)
        mx = jnp.maximum(jnp.maximum(l0, l1), l2)
        e0, e1, e2 = jnp.exp(l0 - mx), jnp.exp(l1 - mx), jnp.exp(l2 - mx)
        inv = 1.0 / (e0 + e1 + e2)
        y_att = jnp.concatenate([o0_ref[rows, :].astype(F32) * (e0 * inv),
                                 il(0) * (e1 * inv),
                                 il(4) * (e2 * inv)], axis=1)
        hga = ga_ref[rows, :].astype(F32)
        y_att = (y_att * (hga * (jnp.tanh(hga) + 1.0))).astype(BF16)
        ya = jnp.dot(y_att, wa_ref[...], preferred_element_type=F32)
        yg = jnp.dot(yg_ref[rows, :], wg_ref[...], preferred_element_type=F32)
        return ya, yg

    def merge_part(rows, ya, yg):
        t_att = jnp.tanh(gates_ref[rows, 0:D_MODEL].astype(F32))
        t_gla = jnp.tanh(gates_ref[rows, D_MODEL:2 * D_MODEL].astype(F32))
        merged = ya * (t_att + 1.0) + yg * (t_gla + 1.0)
        return jnp.dot(merged.astype(BF16), wo_ref[...], preferred_element_type=F32)

    def norm_part(rows, proj):
        h = DEEPNORM_ALPHA * x_ref[rows, :] + proj
        mu = jnp.mean(h, axis=-1, keepdims=True)
        hc = h - mu
        var = jnp.mean(hc * hc, axis=-1, keepdims=True)
        out_ref[rows, :] = hc * lax.rsqrt(var + LN_EPS) * lng_ref[...] + lnb_ref[...]

    n_blk = tm // rb
    blk = [slice(t * rb, t * rb + rb) for t in range(n_blk)]
    stage1, stage2 = {}, {}
    for step in range(n_blk + 2):
        if step < n_blk:
            stage1[step] = attention_part(blk[step])
        if 0 <= step - 1 < n_blk:
            stage2[step - 1] = merge_part(blk[step - 1], *stage1.pop(step - 1))
        if 0 <= step - 2 < n_blk:
            norm_part(blk[step - 2], stage2.pop(step - 2))


def _finalize(x2, o0, l0, o1, l1, o2, l2, ga, y_gla, gates, wa, wg, wo, lng, lnb, *, bsz):
    n_tok, dm = x2.shape
    tm = MERGE_TILE
    nt = n_tok // bsz // tm
    gw = GROUP_WIDTH
    rmap = lambda b, i: (b * nt + i, 0)

    def rows(arr):
        return arr, pl.BlockSpec((tm, arr.shape[-1]), rmap)

    def slabs(arr):
        dil = arr.shape[1]
        return arr, pl.BlockSpec((None, dil, tm // dil, arr.shape[-1]), lambda b, i: (b, 0, i, 0))

    def const(arr):
        return arr, pl.BlockSpec(arr.shape, lambda b, i: (0,) * arr.ndim)

    pairs = [rows(x2), rows(o0.reshape(n_tok, gw)), rows(l0.reshape(n_tok, gw)), slabs(o1), slabs(l1),
             slabs(o2), slabs(l2), rows(ga), rows(y_gla), rows(gates),
             const(wa), const(wg), const(wo), const(lng), const(lnb)]
    return pl.pallas_call(
        functools.partial(_final_body, tm=tm, rb=MERGE_ROW_BLOCK),
        grid=(bsz, nt),
        in_specs=[s for _, s in pairs],
        out_specs=pl.BlockSpec((tm, dm), rmap),
        out_shape=jax.ShapeDtypeStruct((n_tok, dm), F32),
        scratch_shapes=[pltpu.VMEM((8, tm, 128), F32)],
        compiler_params=_params(2),
        name="merge_out",
    )(*[a for a, _ in pairs])


def _layer(x, w_in, gate_w2, gate_b, gla_norm_g, rel_bias, w_att_out, w_gla_out, w_out, ln_g, ln_b):
    bsz, seq, dm = x.shape
    gw = GROUP_WIDTH
    n_tok = bsz * seq

    def qkv_cols(g):
        return [w_in[:, c + gw * g:c + gw * g + gw] for c in (_C_QA, _C_KA, _C_VA)]

    lr_cols = jnp.pad(w_in[:, _C_LR:_C_GATES], ((0, 0), (0, LR_PAD - 2 * GLA_GATE_RANK)))
    w_qk_gla = jnp.concatenate([w_in[:, _C_QB:_C_KB] * (GLA_DK ** -0.5), w_in[:, _C_KB:_C_VB]], axis=1)
    w_nat = jnp.concatenate(qkv_cols(0) + [w_qk_gla, w_in[:, _C_VB:_C_GB], 0.5 * w_in[:, _C_GB:_C_LR],
                                           lr_cols, 0.5 * w_in[:, _C_GA:_C_QB], 0.5 * w_in[:, _C_GATES:_C_END]],
                            axis=1).astype(BF16)
    w_d4 = jnp.concatenate(qkv_cols(1), axis=1).astype(BF16)
    w_d16 = jnp.concatenate(qkv_cols(2), axis=1).astype(BF16)
    qkv0, qk, vv, gb, lr, ga, gates, qkv1, qkv2 = _project(x, w_nat, w_d4, w_d16)

    lane = np.arange(128)
    hmask_att = jnp.asarray(np.stack([lane < 64, lane >= 64]).astype(np.float32) * ATT_HEAD_DIM ** -0.5, BF16)
    bias_all = _attn_bias(rel_bias)
    outs = []
    for g, (qkv, tq, slabs) in enumerate(((qkv0.reshape(bsz, 1, seq, ATT_WIDTH), 2048, 1), (qkv1, 2048, 1),
                                          (qkv2, 512, 4))):
        outs.append(_attention(qkv, bias_all, g, hmask_att, tq, slabs, "attn_d%d" % ATT_GROUPS[g][1]))
    (o0, l0), (o1, l1), (o2, l2) = outs

    lane = np.arange(2 * GLA_DK)
    hmask_gla = jnp.asarray(np.stack([lane < GLA_DK, lane >= GLA_DK]).astype(np.float32), BF16)
    ii = np.arange(GLA_CHUNK)
    tri_f = jnp.asarray((ii[None, :] <= ii[:, None]).astype(np.float32), BF16)
    tri_b = jnp.asarray((ii[None, :] >= ii[:, None]).astype(np.float32), BF16)

    def w2_padded(direction):
        lo = GLA_GATE_RANK * direction
        return jnp.pad(gate_w2[direction], ((lo, LR_PAD - lo - GLA_GATE_RANK), (0, 0))).astype(BF16)

    o_bwd = _gla_pass(qk, vv, lr, w2_padded(1), gate_b[1][None, :].astype(F32), tri_b, hmask_gla, None,
                      reverse=True, final=False, bsz=bsz, name="gla_bwd")
    y_gla = _gla_pass(qk, vv, lr, w2_padded(0), gate_b[0][None, :].astype(F32), tri_f, hmask_gla,
                      (gb, o_bwd, gla_norm_g[None, :].astype(F32)),
                      reverse=False, final=True, bsz=bsz, name="gla_fwd")

    out = _finalize(x.reshape(n_tok, dm), o0, l0, o1, l1, o2, l2, ga, y_gla, gates,
                    w_att_out.astype(BF16), w_gla_out.astype(BF16), (0.5 * w_out).astype(BF16),
                    ln_g[None, :].astype(F32), ln_b[None, :].astype(F32), bsz=bsz)
    return out.reshape(bsz, seq, dm)


def kernel(x, w_in, gla_gate_w2, gla_gate_b, gla_norm_g, rel_bias, w_att_out, w_gla_out, w_out, ln_g, ln_b):
    for layer in range(DEPTH):
        x = _layer(x, w_in[layer], gla_gate_w2[layer], gla_gate_b[layer], gla_norm_g[layer], rel_bias,
                   w_att_out[layer], w_gla_out[layer], w_out[layer], ln_g[layer], ln_b[layer])
    return x
```

```python
import functools

import numpy as np
import jax
import jax.numpy as jnp
from jax import lax
from jax.experimental import pallas as pl
from jax.experimental.pallas import tpu as pltpu

F32 = jnp.float32
BF16 = jnp.bfloat16

D_MODEL = 1024
ATT_GROUPS = ((128, 1), (512, 4), (2048, 16))
HEADS_PER_GROUP = 4
ATT_HEAD_DIM = 64
GROUP_WIDTH = HEADS_PER_GROUP * ATT_HEAD_DIM
ATT_WIDTH = GROUP_WIDTH * len(ATT_GROUPS)
ATT_HALF = 64
NEG_INF = -1e30
GLA_HEADS = 4
GLA_DK = 64
GLA_DV = 128
GLA_KEY_WIDTH = GLA_HEADS * GLA_DK
GLA_VAL_WIDTH = GLA_HEADS * GLA_DV
GLA_GATE_RANK = 16
GLA_TAU = 16.0
GLA_CHUNK = 64
REL_BUCKETS = 32
REL_MAX_DISTANCE = 1024
DEPTH = 1
DEEPNORM_ALPHA = (2.0 * DEPTH) ** 0.25
LN_EPS = 1e-5
RMS_EPS = 1e-6

_C_QA, _C_KA, _C_VA, _C_GA = 0, 768, 1536, 2304
_C_QB, _C_KB, _C_VB, _C_GB = 3072, 3328, 3584, 4096
_C_LR, _C_GATES, _C_END = 4608, 4640, 6688
LR_PAD = 128

ATT_QBLK = 128
ATT_KWIN = 256
ATT_UNROLL = 2
BIAS_SPAN = 512
PROJ_TILE = 1024
PROJ_COL_CHUNK = 1024
GLA_TILE = 1024
MERGE_TILE = 1024
MERGE_ROW_BLOCK = 256

VMEM_LIMIT = 56 * 1024 * 1024


def _params(n_axes):
    return pltpu.CompilerParams(dimension_semantics=("arbitrary",) * n_axes,
                                vmem_limit_bytes=VMEM_LIMIT)


def _split3(a):
    a0 = a.astype(BF16)
    r1 = a - a0.astype(F32)
    a1 = r1.astype(BF16)
    a2 = (r1 - a1.astype(F32)).astype(BF16)
    return a0, a1, a2


_NAT_WIDTHS = (ATT_WIDTH, 512, 512, 512, LR_PAD, ATT_WIDTH, 2 * D_MODEL)
_NAT_DTYPES = (BF16, BF16, BF16, BF16, F32, BF16, BF16)
_NAT_PIECES = (
    (0, 0, GROUP_WIDTH, _C_QA, 1.0, None), (0, 256, GROUP_WIDTH, _C_KA, 1.0, None), (0, 512, GROUP_WIDTH, _C_VA, 1.0, None),
    (1, 0, GLA_KEY_WIDTH, _C_QB, GLA_DK ** -0.5, None), (1, 256, GLA_KEY_WIDTH, _C_KB, 1.0, None),
    (2, 0, GLA_VAL_WIDTH, _C_VB, 1.0, None),
    (3, 0, GLA_VAL_WIDTH, _C_GB, 0.5, "silu"),
    (4, 0, LR_PAD, _C_LR, 1.0, None),
    (5, 0, ATT_WIDTH, _C_GA, 0.5, "silu"),
)
_GATE_CHUNK = 512


def _proj_epilogue(acc, scale, act):
    if scale != 1.0:
        acc = acc * scale
    if act == "silu":
        acc = acc * (jnp.tanh(acc) + 1.0)
    elif act == "sig2":
        acc = jnp.tanh(acc) + 1.0
    return acc


def _proj_body(*refs, tm):
    n_xb = D_MODEL // 128
    n_np = len(_NAT_PIECES)
    x_refs = refs[:n_xb]
    nat_w = refs[n_xb:n_xb + n_np]
    wgate_ref = refs[n_xb + n_np]
    dil_w = refs[n_xb + n_np + 1:n_xb + n_np + 7]
    o_refs = refs[n_xb + n_np + 7:]
    nat_refs, qkv1_ref, qkv2_ref = o_refs[:-2], o_refs[-2], o_refs[-1]
    x = jnp.concatenate([xr[...].astype(BF16) for xr in x_refs], axis=1)
    for w_ref, (oi, ocol, width, _, scale, act) in zip(nat_w, _NAT_PIECES):
        acc = jnp.dot(x, w_ref[...], preferred_element_type=F32)
        nat_refs[oi][:, ocol:ocol + width] = _proj_epilogue(acc, scale, act).astype(nat_refs[oi].dtype)
    for c in range(0, 2 * D_MODEL, _GATE_CHUNK):
        acc = jnp.dot(x, wgate_ref[:, c:c + _GATE_CHUNK], preferred_element_type=F32)
        nat_refs[6][:, c:c + _GATE_CHUNK] = _proj_epilogue(acc, 0.5, "sig2").astype(BF16)
    for g, (dil, o_ref) in enumerate(((4, qkv1_ref), (16, qkv2_ref))):
        n = tm // dil
        xs = jnp.concatenate(
            [jnp.concatenate([xr[pl.ds(r, n, stride=dil), :].astype(BF16) for xr in x_refs], axis=1)
             for r in range(dil)], axis=0)
        for j in range(3):
            c = GROUP_WIDTH * j
            acc = jnp.dot(xs, dil_w[3 * g + j][...], preferred_element_type=F32).astype(o_ref.dtype)
            for r in range(dil):
                o_ref[r, :, c:c + GROUP_WIDTH] = acc[r * n:(r + 1) * n]


def _project(x, w_all, w_gates):
    bsz, seq, dm = x.shape
    tm = PROJ_TILE
    nt = seq // tm
    n_tok = bsz * seq

    def wcols(col, width):
        assert col % width == 0
        return pl.BlockSpec((dm, width), lambda b, i: (0, col // width), pipeline_mode=pl.Buffered(1))

    w_specs = [wcols(col, width) for _, _, width, col, _, _ in _NAT_PIECES]
    w_specs.append(pl.BlockSpec(w_gates.shape, lambda b, i: (0, 0), pipeline_mode=pl.Buffered(1)))
    for g in (1, 2):
        w_specs += [wcols(c + GROUP_WIDTH * g, GROUP_WIDTH) for c in (_C_QA, _C_KA, _C_VA)]
    out_specs = [pl.BlockSpec((tm, n), lambda b, i: (b * nt + i, 0)) for n in _NAT_WIDTHS]
    out_shape = [jax.ShapeDtypeStruct((n_tok, n), dt) for n, dt in zip(_NAT_WIDTHS, _NAT_DTYPES)]
    for dil in (4, 16):
        out_specs.append(pl.BlockSpec((None, dil, tm // dil, ATT_WIDTH), lambda b, i: (b, 0, i, 0)))
        out_shape.append(jax.ShapeDtypeStruct((bsz, dil, seq // dil, ATT_WIDTH), BF16))
    x2 = x.reshape(n_tok, dm)
    x_specs = [pl.BlockSpec((tm, 128), lambda b, i, j=j: (b * nt + i, j)) for j in range(dm // 128)]
    n_w = len(w_specs) - 1
    return pl.pallas_call(
        functools.partial(_proj_body, tm=tm),
        grid=(bsz, nt),
        in_specs=x_specs + w_specs,
        out_specs=out_specs,
        out_shape=out_shape,
        compiler_params=_params(2),
        name="proj",
    )(*([x2] * len(x_specs)), *([w_all] * len(_NAT_PIECES)), w_gates, *([w_all] * (n_w - len(_NAT_PIECES))))


def _t5_bucket(rel):
    half = REL_BUCKETS // 2
    max_exact = half // 2
    ret = (rel > 0).astype(np.int32) * half
    n = np.abs(rel)
    large = max_exact + (np.log(np.maximum(n, 1) / max_exact)
                         / np.log(REL_MAX_DISTANCE / max_exact) * (half - max_exact)).astype(np.int32)
    large = np.minimum(large, half - 1)
    return ret + np.where(n < max_exact, n, large)


def _bucket_onehot():
    m = np.arange(BIAS_SPAN)
    dl = np.where(m < BIAS_SPAN // 2, m, m - BIAS_SPAN)
    out = np.zeros((len(ATT_GROUPS), REL_BUCKETS, BIAS_SPAN), np.float32)
    for g, (_, dilation) in enumerate(ATT_GROUPS):
        out[g, _t5_bucket(dl * dilation), m] = 1.0
    return out


def _bias_body(rb_ref, oh_ref, out_ref):
    oh = oh_ref[...]
    table = sum(jnp.dot(t, oh, preferred_element_type=F32) for t in _split3(rb_ref[...]))
    m = lax.broadcasted_iota(jnp.int32, table.shape, 1)
    table = jnp.where((m <= ATT_HALF) | (m >= BIAS_SPAN - ATT_HALF), table, NEG_INF)
    for h in range(HEADS_PER_GROUP):
        rows = jnp.broadcast_to(table[h:h + 1, :], (ATT_QBLK, BIAS_SPAN))
        for kind in range(3):
            tile = pltpu.roll(rows, kind * ATT_HALF, 1, stride=1, stride_axis=0)
            out_ref[kind, h // 2, ATT_QBLK * (h % 2):ATT_QBLK * (h % 2 + 1), :] = tile[:, :ATT_KWIN]


def _attn_bias(rel_bias):
    n_g = len(ATT_GROUPS)
    rb = jnp.transpose(rel_bias.astype(F32)).reshape(n_g, HEADS_PER_GROUP, REL_BUCKETS)
    rb = jnp.pad(rb, ((0, 0), (0, 8 - HEADS_PER_GROUP), (0, 0)))
    onehot = jnp.asarray(_bucket_onehot(), BF16)
    return pl.pallas_call(
        _bias_body,
        grid=(n_g,),
        in_specs=[pl.BlockSpec((None, 8, REL_BUCKETS), lambda g: (g, 0, 0)),
                  pl.BlockSpec((None, REL_BUCKETS, BIAS_SPAN), lambda g: (g, 0, 0))],
        out_specs=pl.BlockSpec((None, 3, 2, 2 * ATT_QBLK, ATT_KWIN), lambda g: (g, 0, 0, 0, 0)),
        out_shape=jax.ShapeDtypeStruct((n_g, 3, 2, 2 * ATT_QBLK, ATT_KWIN), F32),
        compiler_params=_params(1),
        name="attn_bias",
    )(rb, onehot)


def _attn_body(qa_ref, ka_ref, va_ref, bias_ref, hm_ref, oa_ref, lsea_ref, *, sub, tq, nb):
    qt = pl.program_id(2)
    lo = lax.broadcasted_iota(jnp.int32, (ATT_QBLK, 128), 1) < ATT_HEAD_DIM
    hm0 = hm_ref[0:1, :]
    hm1 = hm_ref[1:2, :]
    for slab in range(qa_ref.shape[0]):
        _attn_slab(qa_ref.at[slab], ka_ref.at[slab], va_ref.at[slab], bias_ref, oa_ref.at[slab], lsea_ref.at[slab],
                   qt=qt, lo=lo, hm0=hm0, hm1=hm1, sub=sub, tq=tq, nb=nb)


def _attn_slab(q_ref, k_ref, v_ref, bias_ref, o_ref, lse_ref, *, qt, lo, hm0, hm1, sub, tq, nb):
    def step(jj, carry):
        work = []
        for u in range(nb):
            j = jj * nb + u
            q0 = qt * tq + j * ATT_QBLK
            ws = pl.multiple_of(jnp.clip(q0 - ATT_HALF, 0, sub - ATT_KWIN), ATT_HALF)
            kind = jnp.where(q0 == 0, 0, jnp.where(q0 == sub - ATT_QBLK, 2, 1))
            r0 = pl.multiple_of(j * ATT_QBLK, ATT_QBLK)
            for p in range(2):
                work.append((r0, ws, kind, p, slice(128 * p, 128 * p + 128)))
        scores = []
        for r0, ws, kind, p, cols in work:
            q = q_ref[pl.ds(r0, ATT_QBLK), cols]
            qs = jnp.concatenate([q * hm0, q * hm1], axis=0)
            k = k_ref[pl.ds(ws, ATT_KWIN), cols]
            s = lax.dot_general(qs, k, (((1,), (1,)), ((), ())), preferred_element_type=F32)
            scores.append(s + bias_ref[kind, p])
        probs = []
        for s in scores:
            m = jnp.max(s, axis=-1, keepdims=True)
            e = jnp.exp(s - m)
            probs.append((e.astype(BF16), m, jnp.sum(e, axis=-1, keepdims=True)))
        outs = []
        for (r0, ws, kind, p, cols), (e, m, den) in zip(work, probs):
            v = v_ref[pl.ds(ws, ATT_KWIN), cols]
            outs.append(jnp.dot(e, v, preferred_element_type=F32))
        for (r0, ws, kind, p, cols), (e, m, den), pv in zip(work, probs, outs):
            pv = pv / den
            lse = m + jnp.log(den)
            o_ref[pl.ds(r0, ATT_QBLK), cols] = jnp.where(lo, pv[:ATT_QBLK], pv[ATT_QBLK:]).astype(o_ref.dtype)
            lse_ref[pl.ds(r0, ATT_QBLK), cols] = jnp.where(lo, lse[:ATT_QBLK], lse[ATT_QBLK:])
        return carry

    lax.fori_loop(0, tq // (ATT_QBLK * nb), step, 0)


def _attention(qkv, bias_all, g, hmask, tq, slabs, name):
    bsz, dil, sub, _ = qkv.shape
    gw = GROUP_WIDTH
    return pl.pallas_call(
        functools.partial(_attn_body, sub=sub, tq=tq, nb=ATT_UNROLL),
        grid=(bsz, dil // slabs, sub // tq),
        in_specs=[pl.BlockSpec((None, slabs, tq, gw), lambda b, r, i: (b, r, i, 0)),
                  pl.BlockSpec((None, slabs, sub, gw), lambda b, r, i: (b, r, 0, 1)),
                  pl.BlockSpec((None, slabs, sub, gw), lambda b, r, i: (b, r, 0, 2)),
                  pl.BlockSpec((None,) + bias_all.shape[1:], lambda b, r, i: (g, 0, 0, 0, 0)),
                  pl.BlockSpec(hmask.shape, lambda b, r, i: (0, 0))],
        out_specs=[pl.BlockSpec((None, slabs, tq, gw), lambda b, r, i: (b, r, i, 0)),
                   pl.BlockSpec((None, slabs, tq, gw), lambda b, r, i: (b, r, i, 0))],
        out_shape=[jax.ShapeDtypeStruct((bsz, dil, sub, gw), BF16),
                   jax.ShapeDtypeStruct((bsz, dil, sub, gw), F32)],
        compiler_params=_params(3),
        name=name,
    )(qkv, qkv, qkv, bias_all, hmask)


def _gla_body(*refs, reverse, final, tile):
    if final:
        (qk_ref, v_ref, lr_ref, w2_ref, gbias_ref, tri_ref, pm_ref, gate_ref, obwd_ref, gnorm_ref,
         out_ref, st_ref, wb_ref, la_ref, kv_ref, dcol_ref, b_ref, qd_ref, kd_ref, kr_ref, sc_ref, o_ref) = refs
    else:
        (qk_ref, v_ref, lr_ref, w2_ref, gbias_ref, tri_ref, pm_ref,
         out_ref, st_ref, wb_ref, la_ref, kv_ref, dcol_ref, b_ref, qd_ref, kd_ref, kr_ref, sc_ref) = refs
    C = GLA_CHUNK
    n_chunks = tile // C
    n_pairs = GLA_HEADS // 2
    pw = 2 * GLA_DK
    pv = 2 * GLA_DV

    @pl.when(pl.program_id(1) == 0)
    def _():
        st_ref[...] = jnp.zeros_like(st_ref)
        wb_ref[...] = jnp.zeros_like(wb_ref)

    z = jnp.dot(lr_ref[...].astype(BF16), w2_ref[...], preferred_element_type=F32) + gbias_ref[...]
    log_a = (jnp.minimum(z, 0.0) - jnp.log(1.0 + jnp.exp(-jnp.abs(z)))) * (1.0 / GLA_TAU)
    la_hi = log_a.astype(BF16)
    la_ref[...] = jnp.concatenate([la_hi, (log_a - la_hi.astype(F32)).astype(BF16)], axis=1)

    row = lax.broadcasted_iota(jnp.int32, (C, pw), 0)
    col = lax.broadcasted_iota(jnp.int32, (C, pw), 1) & (C - 1)
    causal = (col >= row) if reverse else (col <= row)
    same_head = ((lax.broadcasted_iota(jnp.int32, (pw, pv), 0) >= GLA_DK)
                 == (lax.broadcasted_iota(jnp.int32, (pw, pv), 1) >= GLA_DV))
    pm0, pm1 = pm_ref[0:1, :], pm_ref[1:2, :]
    order = [(n_chunks - 1 - i) if reverse else i for i in range(n_chunks)]

    chunk_rows = [slice(c * C, c * C + C) for c in order]
    pair_lanes = [slice(pw * p, pw * p + pw) for p in range(n_pairs)]

    for rows in chunk_rows:
        cs = jnp.dot(tri_ref[...], la_ref[rows, :], preferred_element_type=F32)
        b_ref[rows, :] = cs[:, 0:GLA_KEY_WIDTH] + cs[:, GLA_KEY_WIDTH:2 * GLA_KEY_WIDTH]

    for i, rows in enumerate(chunk_rows):
        b = b_ref[rows, :]
        b_last = b[0:1, :] if reverse else b[C - 1:C, :]
        decay = jnp.exp(b_last)
        inv_b = jnp.exp(-b)
        k = qk_ref[rows, GLA_KEY_WIDTH:2 * GLA_KEY_WIDTH].astype(F32)
        qd_ref[rows, :] = (qk_ref[rows, 0:GLA_KEY_WIDTH].astype(F32) * jnp.exp(b)).astype(BF16)
        kd_ref[rows, :] = (k * inv_b).astype(BF16)
        kr_ref[rows, :] = (k * (inv_b * decay)).astype(BF16)
        for p, lanes in enumerate(pair_lanes):
            dcol_ref[i, p] = jnp.transpose(jnp.broadcast_to(decay[:, lanes], (pw, pw)))
            for hh in range(2):
                h = 2 * p + hh
                wb_ref[i, p, C * hh:C * hh + C, GLA_DV * hh:GLA_DV * hh + GLA_DV] = (
                    v_ref[rows, GLA_DV * h:GLA_DV * h + GLA_DV])

    for i, rows in enumerate(chunk_rows):
        for p, lanes in enumerate(pair_lanes):
            kd = kd_ref[rows, lanes]
            kbd = jnp.concatenate([kd * pm0, kd * pm1], axis=0)
            s = lax.dot_general(qd_ref[rows, lanes], kbd, (((1,), (1,)), ((), ())), preferred_element_type=F32)
            sc_ref[rows, lanes] = jnp.where(causal, s, 0.0).astype(BF16)
    for i, rows in enumerate(chunk_rows):
        for p, lanes in enumerate(pair_lanes):
            kv = lax.dot_general(kr_ref[rows, lanes], v_ref[rows, pv * p:pv * p + pv], (((0,), (0,)), ((), ())),
                                 preferred_element_type=F32)
            kv_ref[i, p] = jnp.where(same_head, kv, 0.0)

    for i in range(n_chunks):
        for p in range(n_pairs):
            s = st_ref[p]
            wb_ref[i, p, pw:2 * pw, :] = s.astype(BF16)
            d = dcol_ref[i, p]
            st_ref[p] = jnp.concatenate([d, d], axis=1) * s + kv_ref[i, p]

    acc_ref = o_ref if final else out_ref
    for i, rows in enumerate(chunk_rows):
        for p, lanes in enumerate(pair_lanes):
            lhs = jnp.concatenate([sc_ref[rows, lanes], qd_ref[rows, lanes]], axis=1)
            acc_ref[rows, pv * p:pv * p + pv] = jnp.dot(lhs, wb_ref[i, p], preferred_element_type=F32)

    if final:
        for rows in chunk_rows:
            o = o_ref[rows, :] + obwd_ref[rows, :]
            parts = []
            for h in range(GLA_HEADS):
                oh = o[:, GLA_DV * h:GLA_DV * h + GLA_DV]
                ms = jnp.mean(oh * oh, axis=-1, keepdims=True)
                parts.append(oh * lax.rsqrt(ms + RMS_EPS))
            o = jnp.concatenate(parts, axis=1) * gnorm_ref[...]
            out_ref[rows, :] = (o * gate_ref[rows, :].astype(F32)).astype(out_ref.dtype)


def _gla_pass(qk, v, lr, w2p, gbias, tri, hmask, extra, *, reverse, final, bsz, name):
    tile = GLA_TILE
    n_tok = qk.shape[0]
    n_tiles = n_tok // bsz // tile
    n_chunks = tile // GLA_CHUNK
    n_pairs, pw, pv = GLA_HEADS // 2, 2 * GLA_DK, 2 * GLA_DV
    if reverse:
        rmap = lambda b, i: (b * n_tiles + (n_tiles - 1 - i), 0)
    else:
        rmap = lambda b, i: (b * n_tiles + i, 0)
    cmap = lambda b, i: (0, 0)
    vw = GLA_VAL_WIDTH
    in_specs = [pl.BlockSpec((tile, 2 * GLA_KEY_WIDTH), rmap),
                pl.BlockSpec((tile, vw), rmap),
                pl.BlockSpec((tile, LR_PAD), rmap),
                pl.BlockSpec(w2p.shape, cmap), pl.BlockSpec(gbias.shape, cmap),
                pl.BlockSpec(tri.shape, cmap), pl.BlockSpec(hmask.shape, cmap)]
    args = [qk, v, lr, w2p, gbias, tri, hmask]
    if final:
        gate, obwd, gnorm = extra
        in_specs += [pl.BlockSpec((tile, vw), rmap), pl.BlockSpec((tile, vw), rmap),
                     pl.BlockSpec(gnorm.shape, cmap)]
        args += [gate, obwd, gnorm]
    scratch = [pltpu.VMEM((n_pairs, pw, pv), F32),
               pltpu.VMEM((n_chunks, n_pairs, 2 * pw, pv), BF16),
               pltpu.VMEM((tile, 2 * GLA_KEY_WIDTH), BF16),
               pltpu.VMEM((n_chunks, n_pairs, pw, pv), F32),
               pltpu.VMEM((n_chunks, n_pairs, pw, pw), F32),
               pltpu.VMEM((tile, GLA_KEY_WIDTH), F32),
               pltpu.VMEM((tile, GLA_KEY_WIDTH), BF16),
               pltpu.VMEM((tile, GLA_KEY_WIDTH), BF16),
               pltpu.VMEM((tile, GLA_KEY_WIDTH), BF16),
               pltpu.VMEM((tile, GLA_KEY_WIDTH), BF16)]
    if final:
        scratch.append(pltpu.VMEM((tile, vw), F32))
    return pl.pallas_call(
        functools.partial(_gla_body, reverse=reverse, final=final, tile=tile),
        grid=(bsz, n_tiles),
        in_specs=in_specs,
        out_specs=pl.BlockSpec((tile, vw), rmap),
        out_shape=jax.ShapeDtypeStruct((n_tok, vw), BF16 if final else F32),
        scratch_shapes=scratch,
        compiler_params=_params(2),
        name=name,
    )(*args)


def _final_body(x_ref, o0_ref, l0_ref, o1_ref, l1_ref, o2_ref, l2_ref, ga_ref, yg_ref, gates_ref,
                wa_ref, wg_ref, wo_ref, lng_ref, lnb_ref, out_ref, il_ref, *, tm, rb):
    for src_ref, dil, slot in ((o1_ref, 4, 0), (l1_ref, 4, 2), (o2_ref, 16, 4), (l2_ref, 16, 6)):
        for r in range(dil):
            val = src_ref[r].astype(F32)
            for half in range(2):
                il_ref[slot + half, pl.ds(r, tm // dil, stride=dil), :] = val[:, 128 * half:128 * half + 128]

    def attention_part(rows):
        def il(slot):
            return jnp.concatenate([il_ref[slot, rows, :], il_ref[slot + 1, rows, :]], axis=1)
        l0, l1, l2 = l0_ref[rows, :], il(2), il(6)
        mx = jnp.maximum(jnp.maximum(l0, l1), l2)
        e0, e1, e2 = jnp.exp(l0 - mx), jnp.exp(l1 - mx), jnp.exp(l2 - mx)
        den = e0 + e1 + e2
        inv = pl.reciprocal(den, approx=True)
        inv = inv * (2.0 - den * inv)
        y_att = jnp.concatenate([o0_ref[rows, :].astype(F32) * (e0 * inv),
                                 il(0) * (e1 * inv),
                                 il(4) * (e2 * inv)], axis=1)
        y_att = y_att.astype(BF16) * ga_ref[rows, :]
        ya = jnp.dot(y_att, wa_ref[...], preferred_element_type=F32)
        yg = jnp.dot(yg_ref[rows, :], wg_ref[...], preferred_element_type=F32)
        return ya, yg

    def merge_part(rows, ya, yg):
        merged = (ya.astype(BF16) * gates_ref[rows, 0:D_MODEL]
                  + yg.astype(BF16) * gates_ref[rows, D_MODEL:2 * D_MODEL])
        return jnp.dot(merged, wo_ref[...], preferred_element_type=F32)

    def norm_part(rows, proj):
        h = DEEPNORM_ALPHA * x_ref[rows, :] + proj
        mu = jnp.mean(h, axis=-1, keepdims=True)
        hc = h - mu
        var = jnp.mean(hc * hc, axis=-1, keepdims=True)
        out_ref[rows, :] = hc * lax.rsqrt(var + LN_EPS) * lng_ref[...] + lnb_ref[...]

    for t in range(tm // rb):
        rows = slice(t * rb, t * rb + rb)
        norm_part(rows, merge_part(rows, *attention_part(rows)))


def _finalize(x2, o0, l0, o1, l1, o2, l2, ga, y_gla, gates, wa, wg, wo, lng, lnb, *, bsz):
    n_tok, dm = x2.shape
    tm = MERGE_TILE
    nt = n_tok // bsz // tm
    gw = GROUP_WIDTH
    rmap = lambda b, i: (b * nt + i, 0)

    def rows(arr):
        return arr, pl.BlockSpec((tm, arr.shape[-1]), rmap)

    def slabs(arr):
        dil = arr.shape[1]
        return arr, pl.BlockSpec((None, dil, tm // dil, arr.shape[-1]), lambda b, i: (b, 0, i, 0))

    def const(arr):
        return arr, pl.BlockSpec(arr.shape, lambda b, i: (0,) * arr.ndim)

    pairs = [rows(x2), rows(o0.reshape(n_tok, gw)), rows(l0.reshape(n_tok, gw)), slabs(o1), slabs(l1),
             slabs(o2), slabs(l2), rows(ga), rows(y_gla), rows(gates),
             const(wa), const(wg), const(wo), const(lng), const(lnb)]
    return pl.pallas_call(
        functools.partial(_final_body, tm=tm, rb=MERGE_ROW_BLOCK),
        grid=(bsz, nt),
        in_specs=[s for _, s in pairs],
        out_specs=pl.BlockSpec((tm, dm), rmap),
        out_shape=jax.ShapeDtypeStruct((n_tok, dm), F32),
        scratch_shapes=[pltpu.VMEM((8, tm, 128), F32)],
        compiler_params=_params(2),
        name="merge_out",
    )(*[a for a, _ in pairs])


def _layer(x, w_in, gate_w2, gate_b, gla_norm_g, rel_bias, w_att_out, w_gla_out, w_out, ln_g, ln_b):
    bsz, seq, dm = x.shape
    gw = GROUP_WIDTH
    n_tok = bsz * seq

    w_all = w_in.astype(BF16)
    qkv0, qk, vv, gb, lr, ga, gates, qkv1, qkv2 = _project(x, w_all, w_all[:, _C_GATES:_C_END])

    lane = np.arange(128)
    hmask_att = jnp.asarray(np.stack([lane < 64, lane >= 64]).astype(np.float32) * ATT_HEAD_DIM ** -0.5, BF16)
    bias_all = _attn_bias(rel_bias)
    outs = []
    for g, (qkv, tq, slabs) in enumerate(((qkv0.reshape(bsz, 1, seq, ATT_WIDTH), 2048, 1), (qkv1, 2048, 1),
                                          (qkv2, 512, 4))):
        outs.append(_attention(qkv, bias_all, g, hmask_att, tq, slabs, "attn_d%d" % ATT_GROUPS[g][1]))
    (o0, l0), (o1, l1), (o2, l2) = outs

    lane = np.arange(2 * GLA_DK)
    hmask_gla = jnp.asarray(np.stack([lane < GLA_DK, lane >= GLA_DK]).astype(np.float32), BF16)
    ii = np.arange(GLA_CHUNK)
    tri_f = jnp.asarray((ii[None, :] <= ii[:, None]).astype(np.float32), BF16)
    tri_b = jnp.asarray((ii[None, :] >= ii[:, None]).astype(np.float32), BF16)

    def w2_padded(direction):
        lo = GLA_GATE_RANK * direction
        return jnp.pad(gate_w2[direction], ((lo, LR_PAD - lo - GLA_GATE_RANK), (0, 0))).astype(BF16)

    o_bwd = _gla_pass(qk, vv, lr, w2_padded(1), gate_b[1][None, :].astype(F32), tri_b, hmask_gla, None,
                      reverse=True, final=False, bsz=bsz, name="gla_bwd")
    y_gla = _gla_pass(qk, vv, lr, w2_padded(0), gate_b[0][None, :].astype(F32), tri_f, hmask_gla,
                      (gb, o_bwd, gla_norm_g[None, :].astype(F32)),
                      reverse=False, final=True, bsz=bsz, name="gla_fwd")

    out = _finalize(x.reshape(n_tok, dm), o0, l0, o1, l1, o2, l2, ga, y_gla, gates,
                    w_att_out.astype(BF16), w_gla_out.astype(BF16), (0.5 * w_out).astype(BF16),
                    ln_g[None, :].astype(F32), ln_b[None, :].astype(F32), bsz=bsz)
    return out.reshape(bsz, seq, dm)


def kernel(x, w_in, gla_gate_w2, gla_gate_b, gla_norm_g, rel_bias, w_att_out, w_gla_out, w_out, ln_g, ln_b):
    for layer in range(DEPTH):
        x = _layer(x, w_in[layer], gla_gate_w2[layer], gla_gate_b[layer], gla_norm_g[layer], rel_bias,
                   w_att_out[layer], w_gla_out[layer], w_out[layer], ln_g[layer], ln_b[layer])
    return x
```
